```python
import jax, jax.numpy as jnp
from jax import lax
import numpy as np

D_MODEL = 1024
BATCH = 2
SEQ = 16384
DEPTH = 2

N_EVEN = (DEPTH + 1) // 2
N_ODD = DEPTH // 2

POOL_DIM = D_MODEL // 2
POOL_WINDOWS = (2, 4, 8, 16)
N_POOL_GROUPS = len(POOL_WINDOWS)
POOL_GROUP_DIM = POOL_DIM // N_POOL_GROUPS

MLA_HEADS = 8
QK_NOPE_DIM = 64
QK_ROPE_DIM = 32
QK_DIM = QK_NOPE_DIM + QK_ROPE_DIM
V_HEAD_DIM = 64
Q_LORA_RANK = 256
KV_LORA_RANK = 128
ROPE_BASE = 10000.0
Q_BLOCK = 128

EVEN_IN_DIM = POOL_DIM + Q_LORA_RANK + KV_LORA_RANK + QK_ROPE_DIM
EVEN_MIX_DIM = POOL_DIM + MLA_HEADS * V_HEAD_DIM

LRU_WIDTH = D_MODEL
LRU_HEADS = 4
LRU_HEAD_DIM = LRU_WIDTH // LRU_HEADS
CONV_WIDTH = 4
LRU_C = 8.0

MEM_TOKENS = 256
MEM_HEADS = 4
MEM_HEAD_DIM = D_MODEL // MEM_HEADS

D_FF = -(-8 * D_MODEL // (3 * 256)) * 256

RMS_EPS = 1e-6
NEG_INF = -1e30

kernel_name = "hybrid_pool_mla_rglru_memxattn"


def rms_norm(x, g):
    xf = x.astype(jnp.float32)
    y = xf * lax.rsqrt(jnp.mean(xf * xf, axis=-1, keepdims=True) + RMS_EPS)
    return (y * g).astype(x.dtype)


def rope_tables(positions):
    inv_freq = ROPE_BASE ** (-jnp.arange(0, QK_ROPE_DIM, 2, dtype=jnp.float32) / QK_ROPE_DIM)
    ang = positions.astype(jnp.float32)[..., None] * inv_freq
    return jnp.cos(ang), jnp.sin(ang)


def apply_rope(t, cos, sin):
    t1, t2 = jnp.split(t.astype(jnp.float32), 2, axis=-1)
    out = jnp.concatenate([t1 * cos - t2 * sin, t2 * cos + t1 * sin], axis=-1)
    return out.astype(t.dtype)


def pool_mixer(u, pool_w, pool_scale):
    B, S, _ = u.shape
    ug = u.reshape(B, S, N_POOL_GROUPS, POOL_GROUP_DIM)
    uf = ug.astype(jnp.float32)
    csum = jnp.concatenate([jnp.zeros((B, 1, N_POOL_GROUPS, POOL_GROUP_DIM), jnp.float32),
                            jnp.cumsum(uf, axis=1)], axis=1)
    t = jnp.arange(S)
    means = []
    for g, w in enumerate(POOL_WINDOWS):
        lo = jnp.maximum(t + 1 - w, 0)
        win_sum = csum[:, 1:, g] - csum[:, lo, g]
        cnt = jnp.minimum(t + 1, w).astype(jnp.float32)
        means.append(win_sum / cnt[None, :, None])
    pooled = (jnp.stack(means, axis=2) - uf).astype(u.dtype)
    y = jnp.einsum('bsgc,gcd->bsgd', pooled, pool_w).reshape(B, S, POOL_DIM)
    return y * pool_scale.astype(y.dtype)


def mla_causal_attention(q_nope, q_rope, k_nope, k_rope, v):
    B, S, H, _ = q_nope.shape
    nb = S // Q_BLOCK
    qn = q_nope.reshape(B, nb, Q_BLOCK, H, QK_NOPE_DIM).transpose(1, 0, 2, 3, 4)
    qr = q_rope.reshape(B, nb, Q_BLOCK, H, QK_ROPE_DIM).transpose(1, 0, 2, 3, 4)
    starts = jnp.arange(nb, dtype=jnp.int32) * Q_BLOCK
    kpos = jnp.arange(S, dtype=jnp.int32)
    scale = QK_DIM ** -0.5

    def one_block(args):
        qn_b, qr_b, start = args
        s = (jnp.einsum('bqhd,bkhd->bhqk', qn_b, k_nope).astype(jnp.float32)
             + jnp.einsum('bqhr,bkr->bhqk', qr_b, k_rope).astype(jnp.float32)) * scale
        qpos = start + jnp.arange(Q_BLOCK, dtype=jnp.int32)
        mask = kpos[None, :] <= qpos[:, None]
        s = jnp.where(mask[None, None], s, NEG_INF)
        p = jax.nn.softmax(s, axis=-1).astype(v.dtype)
        return jnp.einsum('bhqk,bkhd->bqhd', p, v)

    out = lax.map(one_block, (qn, qr, starts))
    return out.transpose(1, 0, 2, 3, 4).reshape(B, S, H * V_HEAD_DIM)


def even_mixer(h, cos, sin, w_in, pool_w, pool_scale, q_norm, w_q_up, kv_norm, w_kv_up, w_out):
    B, S, _ = h.shape
    z = h @ w_in
    u, cq, ckv, kr = jnp.split(z, [POOL_DIM, POOL_DIM + Q_LORA_RANK,
                                   POOL_DIM + Q_LORA_RANK + KV_LORA_RANK], axis=-1)
    y_pool = pool_mixer(u, pool_w, pool_scale)
    q = (rms_norm(cq, q_norm) @ w_q_up).reshape(B, S, MLA_HEADS, QK_DIM)
    q_nope, q_rope = jnp.split(q, [QK_NOPE_DIM], axis=-1)
    kv = (rms_norm(ckv, kv_norm) @ w_kv_up).reshape(B, S, MLA_HEADS, QK_NOPE_DIM + V_HEAD_DIM)
    k_nope, v = jnp.split(kv, [QK_NOPE_DIM], axis=-1)
    q_rope = apply_rope(q_rope, cos[:, :, None, :], sin[:, :, None, :])
    k_rope = apply_rope(kr, cos, sin)
    y_att = mla_causal_attention(q_nope, q_rope, k_nope, k_rope, v)
    return jnp.concatenate([y_pool, y_att], axis=-1) @ w_out


def causal_depthwise_conv(xb, conv_w, conv_b):
    y = lax.conv_general_dilated(xb, conv_w[:, None, :].astype(xb.dtype), window_strides=(1,),
                                 padding=((CONV_WIDTH - 1, 0),),
                                 dimension_numbers=('NWC', 'WIO', 'NWC'),
                                 feature_group_count=xb.shape[-1])
    return y + conv_b.astype(y.dtype)


def linear_scan_combine(c1, c2):
    a1, b1 = c1
    a2, b2 = c2
    return a1 * a2, a2 * b1 + b2


def odd_mixer(h, reset, w_in, conv_w, conv_b, w_rgate, b_rgate, w_igate, b_igate, lam, w_out):
    B, S, _ = h.shape
    z = h @ w_in
    gate_branch, xb = jnp.split(z, [LRU_WIDTH], axis=-1)
    xb = causal_depthwise_conv(xb, conv_w, conv_b)
    xg = xb.reshape(B, S, LRU_HEADS, LRU_HEAD_DIM)
    r = jax.nn.sigmoid(jnp.einsum('bshc,hcd->bshd', xg, w_rgate).reshape(B, S, LRU_WIDTH) + b_rgate)
    i = jax.nn.sigmoid(jnp.einsum('bshc,hcd->bshd', xg, w_igate).reshape(B, S, LRU_WIDTH) + b_igate)
    log_a = -LRU_C * r.astype(jnp.float32) * jax.nn.softplus(-lam.astype(jnp.float32))
    a = jnp.exp(log_a)
    mult = jnp.sqrt(jnp.maximum(-jnp.expm1(2.0 * log_a), 0.0))
    a = jnp.where(reset, 0.0, a)
    mult = jnp.where(reset, 1.0, mult)
    b = mult * (i * xb).astype(jnp.float32)
    _, hseq = lax.associative_scan(linear_scan_combine, (a, b), axis=1)
    y = jax.nn.gelu(gate_branch) * hseq.astype(h.dtype)
    return y @ w_out


def mem_cross_attention(h, mem, norm_mem, w_q, w_kv, w_o):
    B, S, _ = h.shape
    m = rms_norm(mem, norm_mem)
    q = (h @ w_q).reshape(B, S, MEM_HEADS, MEM_HEAD_DIM)
    k, v = jnp.split(m @ w_kv, 2, axis=-1)
    k = k.reshape(B, -1, MEM_HEADS, MEM_HEAD_DIM)
    v = v.reshape(B, -1, MEM_HEADS, MEM_HEAD_DIM)
    s = jnp.einsum('bqhd,bkhd->bhqk', q, k).astype(jnp.float32) * (MEM_HEAD_DIM ** -0.5)
    p = jax.nn.softmax(s, axis=-1).astype(v.dtype)
    o = jnp.einsum('bhqk,bkhd->bqhd', p, v).reshape(B, S, D_MODEL)
    return o @ w_o


def swiglu(h, w_gate_up, w_down):
    g, u = jnp.split(h @ w_gate_up, 2, axis=-1)
    return (jax.nn.silu(g) * u) @ w_down


def setup_inputs(seed: int = 0) -> dict:
    key = jax.random.key(seed)
    ks = iter(jax.random.split(key, 48))
    f32 = jnp.float32

    def w(shape, fan_in):
        return jax.random.normal(next(ks), shape, f32) * fan_in ** -0.5

    def gain(shape):
        return 1.0 + 0.02 * jax.random.normal(next(ks), shape, f32)

    def bias(shape):
        return 0.02 * jax.random.normal(next(ks), shape, f32)

    E, O, L = N_EVEN, N_ODD, DEPTH
    x = jax.random.normal(next(ks), (BATCH, SEQ, D_MODEL), f32)
    mem = jax.random.normal(next(ks), (BATCH, MEM_TOKENS, D_MODEL), f32)
    positions = jnp.broadcast_to(jnp.arange(SEQ, dtype=jnp.int32), (BATCH, SEQ))
    a_c = jax.random.uniform(next(ks), (O, LRU_WIDTH), f32, 0.9, 0.999)
    s_l = a_c ** (1.0 / LRU_C)
    lam = jnp.log(s_l) - jnp.log1p(-s_l)
    return {
        "x": x,
        "mem": mem,
        "positions": positions,
        "ev_norm": gain((E, D_MODEL)),
        "ev_w_in": w((E, D_MODEL, EVEN_IN_DIM), D_MODEL),
        "ev_pool_w": w((E, N_POOL_GROUPS, POOL_GROUP_DIM, POOL_GROUP_DIM), POOL_GROUP_DIM),
        "ev_pool_scale": gain((E, POOL_DIM)),
        "ev_q_norm": gain((E, Q_LORA_RANK)),
        "ev_w_q_up": w((E, Q_LORA_RANK, MLA_HEADS * QK_DIM), Q_LORA_RANK),
        "ev_kv_norm": gain((E, KV_LORA_RANK)),
        "ev_w_kv_up": w((E, KV_LORA_RANK, MLA_HEADS * (QK_NOPE_DIM + V_HEAD_DIM)), KV_LORA_RANK),
        "ev_w_out": w((E, EVEN_MIX_DIM, D_MODEL), EVEN_MIX_DIM),
        "od_norm": gain((O, D_MODEL)),
        "od_w_in": w((O, D_MODEL, 2 * LRU_WIDTH), D_MODEL),
        "od_conv_w": w((O, CONV_WIDTH, LRU_WIDTH), CONV_WIDTH),
        "od_conv_b": bias((O, LRU_WIDTH)),
        "od_w_rgate": w((O, LRU_HEADS, LRU_HEAD_DIM, LRU_HEAD_DIM), LRU_HEAD_DIM),
        "od_b_rgate": bias((O, LRU_WIDTH)),
        "od_w_igate": w((O, LRU_HEADS, LRU_HEAD_DIM, LRU_HEAD_DIM), LRU_HEAD_DIM),
        "od_b_igate": bias((O, LRU_WIDTH)),
        "od_lambda": lam,
        "od_w_out": w((O, LRU_WIDTH, D_MODEL), LRU_WIDTH),
        "xa_norm_x": gain((L, D_MODEL)),
        "xa_norm_mem": gain((L, D_MODEL)),
        "xa_w_q": w((L, D_MODEL, D_MODEL), D_MODEL),
        "xa_w_kv": w((L, D_MODEL, 2 * D_MODEL), D_MODEL),
        "xa_w_o": w((L, D_MODEL, D_MODEL), D_MODEL),
        "ffn_norm": gain((L, D_MODEL)),
        "ffn_w_gate_up": w((L, D_MODEL, 2 * D_FF), D_MODEL),
        "ffn_w_down": w((L, D_FF, D_MODEL), D_FF),
        "final_norm": gain((D_MODEL,)),
    }


def reference(x, mem, positions,
              ev_norm, ev_w_in, ev_pool_w, ev_pool_scale, ev_q_norm, ev_w_q_up,
              ev_kv_norm, ev_w_kv_up, ev_w_out,
              od_norm, od_w_in, od_conv_w, od_conv_b, od_w_rgate, od_b_rgate,
              od_w_igate, od_b_igate, od_lambda, od_w_out,
              xa_norm_x, xa_norm_mem, xa_w_q, xa_w_kv, xa_w_o,
              ffn_norm, ffn_w_gate_up, ffn_w_down, final_norm):
    cos, sin = rope_tables(positions)
    reset = (positions == 0)[..., None]
    for layer in range(DEPTH):
        j = layer // 2
        if layer % 2 == 0:
            h = rms_norm(x, ev_norm[j])
            x = x + even_mixer(h, cos, sin, ev_w_in[j], ev_pool_w[j], ev_pool_scale[j],
                               ev_q_norm[j], ev_w_q_up[j], ev_kv_norm[j], ev_w_kv_up[j],
                               ev_w_out[j])
        else:
            h = rms_norm(x, od_norm[j])
            x = x + odd_mixer(h, reset, od_w_in[j], od_conv_w[j], od_conv_b[j],
                              od_w_rgate[j], od_b_rgate[j], od_w_igate[j], od_b_igate[j],
                              od_lambda[j], od_w_out[j])
        x = x + mem_cross_attention(rms_norm(x, xa_norm_x[layer]), mem, xa_norm_mem[layer],
                                    xa_w_q[layer], xa_w_kv[layer], xa_w_o[layer])
        x = x + swiglu(rms_norm(x, ffn_norm[layer]), ffn_w_gate_up[layer], ffn_w_down[layer])
    return rms_norm(x, final_norm)
```

```python
import functools
import math

import jax
import jax.numpy as jnp
from jax import lax
from jax.experimental import pallas as pl
from jax.experimental.pallas import tpu as pltpu

F32 = jnp.float32
BF16 = jnp.bfloat16

LANES = 128

POOL_WINDOWS = (2, 4, 8, 16)
POOL_HALO = 16
MLA_HEADS = 8
QK_NOPE_DIM = 64
QK_ROPE_DIM = 32
QK_DIM = QK_NOPE_DIM + QK_ROPE_DIM
V_HEAD_DIM = 64
ROPE_BASE = 10000.0
LRU_HEADS = 4
CONV_WIDTH = 4
CONV_HALO = 8
LRU_C = 8.0
MEM_HEADS = 4
RMS_EPS = 1e-6
NEG_INF = -1e30

VMEM_LIMIT_BYTES = 56 * 1024 * 1024


def _params(*sem):
    return pltpu.CompilerParams(dimension_semantics=sem, vmem_limit_bytes=VMEM_LIMIT_BYTES)


def _rms(x, g):
    return x * lax.rsqrt(jnp.mean(x * x, axis=-1, keepdims=True) + RMS_EPS) * g


def _dot(a, b):
    return jnp.dot(a, b, preferred_element_type=F32)


def _dot_nt(a, b):
    return lax.dot_general(a, b, (((1,), (1,)), ((), ())), preferred_element_type=F32)


def _even_front_kernel(x_ref, pos_ref, norm_ref, w_in_ref, pool_w_ref, pool_scale_ref,
                       qn_ref, wq_ref, kvn_ref, wk_ref, wv_ref, rope_ref,
                       ypool_ref, q_ref, k_ref, v_ref, ext_ref):
    i = pl.program_id(1)
    tm = x_ref.shape[1]
    pool_dim = ypool_ref.shape[2]
    h = _rms(x_ref[0], norm_ref[...]).astype(BF16)
    z = _dot(h, w_in_ref[...])

    @pl.when(i == 0)
    def _():
        ext_ref[0:POOL_HALO, :] = jnp.zeros((POOL_HALO, pool_dim), F32)

    u = z[:, :pool_dim]
    ext_ref[POOL_HALO:POOL_HALO + tm, :] = u
    t = i * tm + lax.broadcasted_iota(jnp.int32, (tm, 1), 0)
    parts = []
    for g, w in enumerate(POOL_WINDOWS):
        cols = slice(g * LANES, (g + 1) * LANES)
        ug = u[:, cols]
        acc = ug
        for k in range(1, w):
            acc = acc + ext_ref[POOL_HALO - k:POOL_HALO - k + tm, cols]
        cnt = jnp.minimum(t + 1, w).astype(F32)
        pooled = acc / cnt - ug
        parts.append(_dot(pooled.astype(BF16), pool_w_ref[g]))
    y_pool = jnp.concatenate(parts, axis=-1) * pool_scale_ref[...]
    ypool_ref[0] = y_pool.astype(BF16)
    ext_ref[0:POOL_HALO, :] = ext_ref[tm:tm + POOL_HALO, :]

    ang = pos_ref[0].astype(F32) * rope_ref[0:1, :]
    cos = jnp.cos(ang)
    sin = jnp.sin(ang)
    sin_lo = sin * rope_ref[1:2, :]
    sin_hi = sin * rope_ref[2:3, :]

    def rope(slab):
        return (slab * cos + pltpu.roll(slab, LANES - QK_ROPE_DIM // 2, 1) * sin_lo
                + pltpu.roll(slab, QK_ROPE_DIM // 2, 1) * sin_hi)

    q_lat = z[:, pool_dim:pool_dim + qn_ref.shape[1]]
    qf = _dot(_rms(q_lat, qn_ref[...]).astype(BF16), wq_ref[...])
    for hd in range(MLA_HEADS):
        q_ref[0, hd] = rope(qf[:, hd * LANES:(hd + 1) * LANES]).astype(BF16)

    kv0 = pool_dim + qn_ref.shape[1]
    kv_lat = _rms(z[:, kv0:kv0 + kvn_ref.shape[1]], kvn_ref[...]).astype(BF16)
    kf = _dot(kv_lat, wk_ref[...])
    vf = _dot(kv_lat, wv_ref[...])
    k_rope = rope(z[:, kv0 + kvn_ref.shape[1]:])
    for hd in range(MLA_HEADS):
        k_ref[0, hd] = (kf[:, hd * LANES:(hd + 1) * LANES] + k_rope).astype(BF16)
    for p in range(MLA_HEADS // 2):
        v_ref[0, p] = vf[:, p * LANES:(p + 1) * LANES].astype(BF16)


def _even_front(x, pos_col, norm, w_in, pool_w, pool_scale, qn, wq, kvn, wk, wv, rope_tab, tm):
    B, S, D = x.shape
    pool_dim = pool_scale.shape[1]
    const = lambda b, i: (0, 0)
    return pl.pallas_call(
        _even_front_kernel,
        grid=(B, S // tm),
        in_specs=[
            pl.BlockSpec((1, tm, D), lambda b, i: (b, i, 0)),
            pl.BlockSpec((1, tm, 1), lambda b, i: (b, i, 0)),
            pl.BlockSpec(norm.shape, const),
            pl.BlockSpec(w_in.shape, const),
            pl.BlockSpec(pool_w.shape, lambda b, i: (0, 0, 0)),
            pl.BlockSpec(pool_scale.shape, const),
            pl.BlockSpec(qn.shape, const),
            pl.BlockSpec(wq.shape, const),
            pl.BlockSpec(kvn.shape, const),
            pl.BlockSpec(wk.shape, const),
            pl.BlockSpec(wv.shape, const),
            pl.BlockSpec(rope_tab.shape, const),
        ],
        out_specs=[
            pl.BlockSpec((1, tm, pool_dim), lambda b, i: (b, i, 0)),
            pl.BlockSpec((1, MLA_HEADS, tm, LANES), lambda b, i: (b, 0, i, 0)),
            pl.BlockSpec((1, MLA_HEADS, tm, LANES), lambda b, i: (b, 0, i, 0)),
            pl.BlockSpec((1, MLA_HEADS // 2, tm, LANES), lambda b, i: (b, 0, i, 0)),
        ],
        out_shape=[
            jax.ShapeDtypeStruct((B, S, pool_dim), BF16),
            jax.ShapeDtypeStruct((B, MLA_HEADS, S, LANES), BF16),
            jax.ShapeDtypeStruct((B, MLA_HEADS, S, LANES), BF16),
            jax.ShapeDtypeStruct((B, MLA_HEADS // 2, S, LANES), BF16),
        ],
        scratch_shapes=[pltpu.VMEM((POOL_HALO + tm, pool_dim), F32)],
        compiler_params=_params("arbitrary", "arbitrary"),
        name="even_front",
    )(x, pos_col, norm, w_in, pool_w, pool_scale, qn, wq, kvn, wk, wv, rope_tab)


def _attn_kernel(q_ref, k_ref, v_ref, o_ref, m_ref, l_ref, acc_ref, *, tq):
    qi = pl.program_id(2)
    c = (QK_DIM ** -0.5) * math.log2(math.e)
    row = lax.broadcasted_iota(jnp.int32, (tq, tq), 0)
    col = lax.broadcasted_iota(jnp.int32, (tq, tq), 1)
    causal = col <= row
    outs = []
    for hh in range(2):
        q = q_ref[0, hh]
        m_ref[...] = jnp.full((tq, 1), NEG_INF, F32)
        l_ref[...] = jnp.zeros((tq, 1), F32)
        acc_ref[...] = jnp.zeros((tq, LANES), F32)

        def step(j, masked):
            r0 = pl.multiple_of(j * tq, tq)
            s = _dot_nt(q, k_ref[0, hh, pl.ds(r0, tq), :])
            if masked:
                s = jnp.where(causal, s, NEG_INF)
            m_old = m_ref[...]
            m_new = jnp.maximum(m_old, jnp.max(s, axis=-1, keepdims=True))
            alpha = jnp.exp2((m_old - m_new) * c)
            p = jnp.exp2((s - m_new) * c)
            l_ref[...] = alpha * l_ref[...] + jnp.sum(p, axis=-1, keepdims=True)
            acc_ref[...] = alpha * acc_ref[...] + _dot(p.astype(BF16), v_ref[0, 0, pl.ds(r0, tq), :])
            m_ref[...] = m_new

        def body(j, carry):
            step(j, False)
            return carry

        lax.fori_loop(0, qi, body, 0)
        step(qi, True)
        outs.append(acc_ref[...] / l_ref[...])
    lane = lax.broadcasted_iota(jnp.int32, (tq, LANES), 1)
    o_ref[0] = jnp.where(lane < V_HEAD_DIM, outs[0], outs[1]).astype(BF16)


def _attention(q, k, v, tq):
    B, H, S, _ = q.shape
    return pl.pallas_call(
        functools.partial(_attn_kernel, tq=tq),
        grid=(B, H // 2, S // tq),
        in_specs=[
            pl.BlockSpec((1, 2, tq, LANES), lambda b, p, i: (b, p, i, 0)),
            pl.BlockSpec((1, 2, S, LANES), lambda b, p, i: (b, p, 0, 0)),
            pl.BlockSpec((1, 1, S, LANES), lambda b, p, i: (b, p, 0, 0)),
        ],
        out_specs=pl.BlockSpec((1, tq, LANES), lambda b, p, i: (b, i, p)),
        out_shape=jax.ShapeDtypeStruct((B, S, (H // 2) * LANES), BF16),
        scratch_shapes=[pltpu.VMEM((tq, 1), F32), pltpu.VMEM((tq, 1), F32),
                        pltpu.VMEM((tq, LANES), F32)],
        compiler_params=_params("arbitrary", "arbitrary", "arbitrary"),
        name="mla_attention",
    )(q, k, v)


def _gelu_tanh(x):
    return 0.5 * x * (1.0 + jnp.tanh(math.sqrt(2.0 / math.pi) * (x + 0.044715 * (x * x * x))))


def _odd_kernel(x_ref, pos_ref, norm_ref, w_in_ref, conv_w_ref, conv_b_ref, wg_ref, bg_ref,
                lam_ref, w_out_ref, out_ref, ext_ref, a_ref, b_ref, hcar_ref):
    i = pl.program_id(1)
    tm = x_ref.shape[1]
    width = lam_ref.shape[1]
    hd = width // LRU_HEADS
    x = x_ref[0]
    z = _dot(_rms(x, norm_ref[...]).astype(BF16), w_in_ref[...])
    gate = z[:, :width]
    xb = z[:, width:]

    @pl.when(i == 0)
    def _():
        ext_ref[0:CONV_HALO, :] = jnp.zeros((CONV_HALO, width), F32)
        hcar_ref[...] = jnp.zeros(hcar_ref.shape, F32)

    ext_ref[CONV_HALO:CONV_HALO + tm, :] = xb
    xc = conv_b_ref[...] + conv_w_ref[CONV_WIDTH - 1:CONV_WIDTH, :] * xb
    for k in range(CONV_WIDTH - 1):
        off = CONV_HALO - (CONV_WIDTH - 1) + k
        xc = xc + conv_w_ref[k:k + 1, :] * ext_ref[off:off + tm, :]
    ext_ref[0:CONV_HALO, :] = ext_ref[tm:tm + CONV_HALO, :]

    xcb = xc.astype(BF16)
    r_parts, i_parts = [], []
    for h in range(LRU_HEADS):
        g = _dot(xcb[:, h * hd:(h + 1) * hd], wg_ref[h])
        r_parts.append(g[:, :hd])
        i_parts.append(g[:, hd:])
    r = jax.nn.sigmoid(jnp.concatenate(r_parts, axis=-1) + bg_ref[0:1, :])
    ig = jax.nn.sigmoid(jnp.concatenate(i_parts, axis=-1) + bg_ref[1:2, :])
    nlam = -lam_ref[...]
    softplus = jnp.maximum(nlam, 0.0) + jnp.log1p(jnp.exp(-jnp.abs(nlam)))
    log_a = -LRU_C * r * softplus
    a = jnp.exp(log_a)
    mult = jnp.sqrt(jnp.maximum(-jnp.tanh(log_a) * (a * a + 1.0), 0.0))
    reset = pos_ref[0] == 0
    a_ref[...] = jnp.where(reset, 0.0, a)
    b_ref[...] = jnp.where(reset, 1.0, mult) * (ig * xc)

    sub = lax.broadcasted_iota(jnp.int32, (8, width), 0)

    def scan_tile(ci, hprev):
        r0 = pl.multiple_of(ci * 8, 8)
        at = a_ref[pl.ds(r0, 8), :]
        bt = b_ref[pl.ds(r0, 8), :]
        for d in (1, 2, 4):
            keep = sub >= d
            a_s = jnp.where(keep, pltpu.roll(at, d, 0), 1.0)
            b_s = jnp.where(keep, pltpu.roll(bt, d, 0), 0.0)
            bt = at * b_s + bt
            at = at * a_s
        hcur = at * hprev + bt
        b_ref[pl.ds(r0, 8), :] = hcur
        return jnp.broadcast_to(hcur[7:8, :], (8, width))

    hcar_ref[...] = lax.fori_loop(0, tm // 8, scan_tile, hcar_ref[...])
    y = _gelu_tanh(gate) * b_ref[...]
    out_ref[0] = x + _dot(y.astype(BF16), w_out_ref[...])


def _odd_mixer(x, pos_col, norm, w_in, conv_w, conv_b, wg, bg, lam, w_out, tm):
    B, S, D = x.shape
    width = lam.shape[1]
    const = lambda b, i: (0, 0)
    return pl.pallas_call(
        _odd_kernel,
        grid=(B, S // tm),
        in_specs=[
            pl.BlockSpec((1, tm, D), lambda b, i: (b, i, 0)),
            pl.BlockSpec((1, tm, 1), lambda b, i: (b, i, 0)),
            pl.BlockSpec(norm.shape, const),
            pl.BlockSpec(w_in.shape, const),
            pl.BlockSpec(conv_w.shape, const),
            pl.BlockSpec(conv_b.shape, const),
            pl.BlockSpec(wg.shape, lambda b, i: (0, 0, 0)),
            pl.BlockSpec(bg.shape, const),
            pl.BlockSpec(lam.shape, const),
            pl.BlockSpec(w_out.shape, const),
        ],
        out_specs=pl.BlockSpec((1, tm, D), lambda b, i: (b, i, 0)),
        out_shape=jax.ShapeDtypeStruct((B, S, D), F32),
        scratch_shapes=[pltpu.VMEM((CONV_HALO + tm, width), F32),
                        pltpu.VMEM((tm, width), F32), pltpu.VMEM((tm, width), F32),
                        pltpu.VMEM((8, width), F32)],
        compiler_params=_params("arbitrary", "arbitrary"),
        name="odd_mixer",
    )(x, pos_col, norm, w_in, conv_w, conv_b, wg, bg, lam, w_out)


def _memkv_kernel(mem_ref, norm_ref, w_ref, out_ref):
    out_ref[0] = _dot(_rms(mem_ref[0], norm_ref[...]).astype(BF16), w_ref[...]).astype(BF16)


def _memkv(mem, norm, w_kv):
    B, M, D = mem.shape
    return pl.pallas_call(
        _memkv_kernel,
        grid=(B,),
        in_specs=[pl.BlockSpec((1, M, D), lambda b: (b, 0, 0)),
                  pl.BlockSpec(norm.shape, lambda b: (0, 0)),
                  pl.BlockSpec(w_kv.shape, lambda b: (0, 0))],
        out_specs=pl.BlockSpec((1, M, w_kv.shape[1]), lambda b: (b, 0, 0)),
        out_shape=jax.ShapeDtypeStruct((B, M, w_kv.shape[1]), BF16),
        compiler_params=_params("arbitrary"),
        name="mem_kv",
    )(mem, norm, w_kv)


def _xattn_kernel(*refs, with_mix):
    if with_mix:
        x_ref, yp_ref, ya_ref, wop_ref, woa_ref, norm_ref, wq_ref, kv_ref, wo_ref, out_ref = refs
        x = x_ref[...] + _dot(yp_ref[...], wop_ref[...]) + _dot(ya_ref[...], woa_ref[...])
    else:
        x_ref, norm_ref, wq_ref, kv_ref, wo_ref, out_ref = refs
        x = x_ref[...]
    d = x.shape[1]
    hd = d // MEM_HEADS
    q = _dot(_rms(x, norm_ref[...]).astype(BF16), wq_ref[...]).astype(BF16)
    outs = []
    for h in range(MEM_HEADS):
        kh = kv_ref[0, :, h * hd:(h + 1) * hd]
        vh = kv_ref[0, :, d + h * hd:d + (h + 1) * hd]
        s = _dot_nt(q[:, h * hd:(h + 1) * hd], kh) * (hd ** -0.5)
        p = jnp.exp(s - jnp.max(s, axis=-1, keepdims=True))
        o = _dot(p.astype(BF16), vh) / jnp.sum(p, axis=-1, keepdims=True)
        outs.append(o.astype(BF16))
    out_ref[...] = x + _dot(jnp.concatenate(outs, axis=-1), wo_ref[...])


def _xattn(x2d, mix, norm, wq, kv, wo, tm, tiles_per_batch):
    T, D = x2d.shape
    const = lambda i: (0, 0)
    row = lambda i: (i, 0)
    args = [x2d]
    specs = [pl.BlockSpec((tm, D), row)]
    if mix is not None:
        yp, ya, wop, woa = mix
        args += [yp, ya, wop, woa]
        specs += [pl.BlockSpec((tm, yp.shape[1]), row), pl.BlockSpec((tm, ya.shape[1]), row),
                  pl.BlockSpec(wop.shape, const), pl.BlockSpec(woa.shape, const)]
    args += [norm, wq, kv, wo]
    specs += [pl.BlockSpec(norm.shape, const), pl.BlockSpec(wq.shape, const),
              pl.BlockSpec((1,) + kv.shape[1:], lambda i: (i // tiles_per_batch, 0, 0)),
              pl.BlockSpec(wo.shape, const)]
    return pl.pallas_call(
        functools.partial(_xattn_kernel, with_mix=mix is not None),
        grid=(T // tm,),
        in_specs=specs,
        out_specs=pl.BlockSpec((tm, D), row),
        out_shape=jax.ShapeDtypeStruct((T, D), F32),
        compiler_params=_params("arbitrary"),
        name="mem_xattn_mix" if mix is not None else "mem_xattn",
    )(*args)


def _ffn_kernel(x_ref, norm_ref, wg_ref, wu_ref, wd_ref, fnorm_ref, out_ref, h_ref, acc_ref,
                *, final):
    f = pl.program_id(1)

    @pl.when(f == 0)
    def _():
        x = x_ref[...]
        h_ref[...] = _rms(x, norm_ref[...]).astype(BF16)
        acc_ref[...] = x

    h = h_ref[...]
    g = _dot(h, wg_ref[...])
    u = _dot(h, wu_ref[...])
    act = (g * jax.nn.sigmoid(g) * u).astype(BF16)
    acc_ref[...] += _dot(act, wd_ref[...])

    @pl.when(f == pl.num_programs(1) - 1)
    def _():
        y = acc_ref[...]
        out_ref[...] = _rms(y, fnorm_ref[...]) if final else y


def _ffn(x2d, norm, w_gate_up, w_down, fnorm, tm, tf, final):
    T, D = x2d.shape
    ff = w_down.shape[0]
    nf = ff // tf
    return pl.pallas_call(
        functools.partial(_ffn_kernel, final=final),
        grid=(T // tm, nf),
        in_specs=[
            pl.BlockSpec((tm, D), lambda i, f: (i, 0)),
            pl.BlockSpec(norm.shape, lambda i, f: (0, 0)),
            pl.BlockSpec((D, tf), lambda i, f: (0, f)),
            pl.BlockSpec((D, tf), lambda i, f: (0, f + nf)),
            pl.BlockSpec((tf, D), lambda i, f: (f, 0)),
            pl.BlockSpec(fnorm.shape, lambda i, f: (0, 0)),
        ],
        out_specs=pl.BlockSpec((tm, D), lambda i, f: (i, 0)),
        out_shape=jax.ShapeDtypeStruct((T, D), F32),
        scratch_shapes=[pltpu.VMEM((tm, D), BF16), pltpu.VMEM((tm, D), F32)],
        compiler_params=_params("arbitrary", "arbitrary"),
        name="ffn_final" if final else "ffn",
    )(x2d, norm, w_gate_up, w_gate_up, w_down, fnorm)


def _tiles(S, ff):
    tm = min(512, S)
    tq = min(512, S)
    tf = ff
    for n in range(2, ff // LANES + 1):
        if ff % (n * LANES) == 0:
            tf = ff // n
            break
    return tm, tq, tf


def _pad_cols(w, groups, width):
    k = w.shape[0]
    w = w.reshape(k, groups, -1)
    return jnp.pad(w, ((0, 0), (0, 0), (0, width - w.shape[2]))).reshape(k, groups * width)


def kernel(x, mem, positions, ev_norm, ev_w_in, ev_pool_w, ev_pool_scale, ev_q_norm, ev_w_q_up, ev_kv_norm, ev_w_kv_up, ev_w_out, od_norm, od_w_in, od_conv_w, od_conv_b, od_w_rgate, od_b_rgate, od_w_igate, od_b_igate, od_lambda, od_w_out, xa_norm_x, xa_norm_mem, xa_w_q, xa_w_kv, xa_w_o, ffn_norm, ffn_w_gate_up, ffn_w_down, final_norm):
    B, S, D = x.shape
    depth = xa_w_q.shape[0]
    ff = ffn_w_down.shape[1]
    tm, tq, tf = _tiles(S, ff)
    pos_col = positions.reshape(B, S, 1)
    pool_dim = ev_pool_scale.shape[1]
    q_rank = ev_q_norm.shape[1]
    kv_rank = ev_kv_norm.shape[1]

    half = QK_ROPE_DIM // 2
    inv_freq = ROPE_BASE ** (-jnp.arange(0, QK_ROPE_DIM, 2, dtype=F32) / QK_ROPE_DIM)
    lo = slice(QK_NOPE_DIM, QK_NOPE_DIM + half)
    hi = slice(QK_NOPE_DIM + half, QK_NOPE_DIM + QK_ROPE_DIM)
    rope_tab = jnp.zeros((8, LANES), F32)
    rope_tab = rope_tab.at[0, lo].set(inv_freq).at[0, hi].set(inv_freq)
    rope_tab = rope_tab.at[1, lo].set(-1.0).at[2, hi].set(1.0)

    row = lambda v: v.reshape(1, -1)
    h = x
    for layer in range(depth):
        j = layer // 2
        if layer % 2 == 0:
            lat = pool_dim + q_rank + kv_rank
            w_in = jnp.concatenate(
                [ev_w_in[j][:, :lat], jnp.zeros((D, QK_NOPE_DIM), F32), ev_w_in[j][:, lat:],
                 jnp.zeros((D, LANES - QK_DIM), F32)], axis=1).astype(BF16)
            wq = _pad_cols(ev_w_q_up[j], MLA_HEADS, LANES).astype(BF16)
            wkv = ev_w_kv_up[j].reshape(kv_rank, MLA_HEADS, QK_NOPE_DIM + V_HEAD_DIM)
            wk = _pad_cols(wkv[:, :, :QK_NOPE_DIM].reshape(kv_rank, -1), MLA_HEADS, LANES).astype(BF16)
            wv = wkv[:, :, QK_NOPE_DIM:].reshape(kv_rank, -1).astype(BF16)
            y_pool, q, k, v = _even_front(
                h, pos_col, row(ev_norm[j]), w_in, ev_pool_w[j].astype(BF16), row(ev_pool_scale[j]),
                row(ev_q_norm[j]), wq, row(ev_kv_norm[j]), wk, wv, rope_tab, tm)
            y_att = _attention(q, k, v, tq)
            w_out = ev_w_out[j].astype(BF16)
            mix = (y_pool.reshape(B * S, -1), y_att.reshape(B * S, -1),
                   w_out[:pool_dim], w_out[pool_dim:])
            h2d = h.reshape(B * S, D)
        else:
            lw = od_lambda.shape[1]
            hd = lw // LRU_HEADS
            wg = jnp.concatenate([od_w_rgate[j], od_w_igate[j]], axis=-1).astype(BF16)
            bg = jnp.stack([od_b_rgate[j], od_b_igate[j]])
            h = _odd_mixer(h, pos_col, row(od_norm[j]), od_w_in[j].astype(BF16), od_conv_w[j],
                           row(od_conv_b[j]), wg, bg, row(od_lambda[j]), od_w_out[j].astype(BF16), tm)
            mix = None
            h2d = h.reshape(B * S, D)
        kv = _memkv(mem, row(xa_norm_mem[layer]), xa_w_kv[layer].astype(BF16))
        h2d = _xattn(h2d, mix, row(xa_norm_x[layer]), xa_w_q[layer].astype(BF16), kv,
                     xa_w_o[layer].astype(BF16), tm, S // tm)
        h2d = _ffn(h2d, row(ffn_norm[layer]), ffn_w_gate_up[layer].astype(BF16),
                   ffn_w_down[layer].astype(BF16), row(final_norm), tm, tf,
                   final=layer == depth - 1)
        h = h2d.reshape(B, S, D)
    return h
```

```python
import functools
import math

import jax
import jax.numpy as jnp
from jax import lax
from jax.experimental import pallas as pl
from jax.experimental.pallas import tpu as pltpu

F32 = jnp.float32
BF16 = jnp.bfloat16

LANES = 128

POOL_WINDOWS = (2, 4, 8, 16)
POOL_HALO = 16
MLA_HEADS = 8
QK_NOPE_DIM = 64
QK_ROPE_DIM = 32
QK_DIM = QK_NOPE_DIM + QK_ROPE_DIM
V_HEAD_DIM = 64
QK_EXP2_SCALE = (QK_DIM ** -0.5) * math.log2(math.e)
ROPE_BASE = 10000.0
LRU_HEADS = 4
CONV_WIDTH = 4
CONV_HALO = 8
LRU_C = 8.0
MEM_HEADS = 4
RMS_EPS = 1e-6
NEG_INF = -1e30

VMEM_LIMIT_BYTES = 56 * 1024 * 1024


def _params(*sem):
    return pltpu.CompilerParams(dimension_semantics=sem, vmem_limit_bytes=VMEM_LIMIT_BYTES)


def _rms(x, g):
    return x * lax.rsqrt(jnp.mean(x * x, axis=-1, keepdims=True) + RMS_EPS) * g


def _dot(a, b):
    return jnp.dot(a, b, preferred_element_type=F32)


def _dot_nt(a, b):
    return lax.dot_general(a, b, (((1,), (1,)), ((), ())), preferred_element_type=F32)


def _even_front_kernel(x_ref, pos_ref, norm_ref, w_in_ref, pool_w_ref, pool_scale_ref,
                       qn_ref, wq_ref, kvn_ref, wk_ref, wv_ref, rope_ref,
                       ypool_ref, q_ref, k_ref, v_ref, ext_ref):
    i = pl.program_id(1)
    tm = x_ref.shape[1]
    pool_dim = ypool_ref.shape[2]
    h = _rms(x_ref[0], norm_ref[...]).astype(BF16)
    z = _dot(h, w_in_ref[...])

    @pl.when(i == 0)
    def _():
        ext_ref[0:POOL_HALO, :] = jnp.zeros((POOL_HALO, pool_dim), F32)

    u = z[:, :pool_dim]
    ext_ref[POOL_HALO:POOL_HALO + tm, :] = u
    t = i * tm + lax.broadcasted_iota(jnp.int32, (tm, 1), 0)
    parts = []
    for g, w in enumerate(POOL_WINDOWS):
        cols = slice(g * LANES, (g + 1) * LANES)
        ug = u[:, cols]
        acc = ug
        for k in range(1, w):
            acc = acc + ext_ref[POOL_HALO - k:POOL_HALO - k + tm, cols]
        cnt = jnp.minimum(t + 1, w).astype(F32)
        pooled = acc / cnt - ug
        parts.append(_dot(pooled.astype(BF16), pool_w_ref[g]))
    y_pool = jnp.concatenate(parts, axis=-1) * pool_scale_ref[...]
    ypool_ref[0] = y_pool.astype(BF16)
    ext_ref[0:POOL_HALO, :] = ext_ref[tm:tm + POOL_HALO, :]

    ang = pos_ref[0].astype(F32) * rope_ref[0:1, :]
    cos = jnp.cos(ang)
    sin = jnp.sin(ang)
    sin_lo = sin * rope_ref[1:2, :]
    sin_hi = sin * rope_ref[2:3, :]

    def rope(slab):
        return (slab * cos + pltpu.roll(slab, LANES - QK_ROPE_DIM // 2, 1) * sin_lo
                + pltpu.roll(slab, QK_ROPE_DIM // 2, 1) * sin_hi)

    q_lat = z[:, pool_dim:pool_dim + qn_ref.shape[1]]
    qf = _dot(_rms(q_lat, qn_ref[...]).astype(BF16), wq_ref[...])
    for hd in range(MLA_HEADS):
        q_ref[0, hd] = (rope(qf[:, hd * LANES:(hd + 1) * LANES]) * QK_EXP2_SCALE).T.astype(BF16)

    kv0 = pool_dim + qn_ref.shape[1]
    kv_lat = _rms(z[:, kv0:kv0 + kvn_ref.shape[1]], kvn_ref[...]).astype(BF16)
    kf = _dot(kv_lat, wk_ref[...])
    vf = _dot(kv_lat, wv_ref[...])
    k_rope = rope(z[:, kv0 + kvn_ref.shape[1]:])
    for hd in range(MLA_HEADS):
        k_ref[0, hd] = (kf[:, hd * LANES:(hd + 1) * LANES] + k_rope).astype(BF16)
    ones_col = rope_ref[3:4, :]
    tk = v_ref.shape[4]
    for hd in range(MLA_HEADS):
        vt = (vf[:, hd * LANES:(hd + 1) * LANES] + ones_col).T.astype(BF16)
        for u in range(tm // tk):
            v_ref[0, hd, u] = vt[:, u * tk:(u + 1) * tk]


def _even_front(x, pos_col, norm, w_in, pool_w, pool_scale, qn, wq, kvn, wk, wv, rope_tab, tm, tk):
    B, S, D = x.shape
    pool_dim = pool_scale.shape[1]
    const = lambda b, i: (0, 0)
    return pl.pallas_call(
        _even_front_kernel,
        grid=(B, S // tm),
        in_specs=[
            pl.BlockSpec((1, tm, D), lambda b, i: (b, i, 0)),
            pl.BlockSpec((1, tm, 1), lambda b, i: (b, i, 0)),
            pl.BlockSpec(norm.shape, const),
            pl.BlockSpec(w_in.shape, const),
            pl.BlockSpec(pool_w.shape, lambda b, i: (0, 0, 0)),
            pl.BlockSpec(pool_scale.shape, const),
            pl.BlockSpec(qn.shape, const),
            pl.BlockSpec(wq.shape, const),
            pl.BlockSpec(kvn.shape, const),
            pl.BlockSpec(wk.shape, const),
            pl.BlockSpec(wv.shape, const),
            pl.BlockSpec(rope_tab.shape, const),
        ],
        out_specs=[
            pl.BlockSpec((1, tm, pool_dim), lambda b, i: (b, i, 0)),
            pl.BlockSpec((1, MLA_HEADS, LANES, tm), lambda b, i: (b, 0, 0, i)),
            pl.BlockSpec((1, MLA_HEADS, tm, LANES), lambda b, i: (b, 0, i, 0)),
            pl.BlockSpec((1, MLA_HEADS, tm // tk, LANES, tk), lambda b, i: (b, 0, i, 0, 0)),
        ],
        out_shape=[
            jax.ShapeDtypeStruct((B, S, pool_dim), BF16),
            jax.ShapeDtypeStruct((B, MLA_HEADS, LANES, S), BF16),
            jax.ShapeDtypeStruct((B, MLA_HEADS, S, LANES), BF16),
            jax.ShapeDtypeStruct((B, MLA_HEADS, S // tk, LANES, tk), BF16),
        ],
        scratch_shapes=[pltpu.VMEM((POOL_HALO + tm, pool_dim), F32)],
        compiler_params=_params("arbitrary", "arbitrary"),
        name="even_front",
    )(x, pos_col, norm, w_in, pool_w, pool_scale, qn, wq, kvn, wk, wv, rope_tab)


def _attn_kernel(qt_ref, k_ref, vt_ref, o_ref, m_ref, acc_ref, *, tq, tk):
    qi = pl.program_id(2)
    sub = tq // tk
    m_ref[...] = jnp.full(m_ref.shape, NEG_INF, F32)
    acc_ref[...] = jnp.zeros(acc_ref.shape, F32)

    def group(g, masked):
        scores = {}
        for u in range(sub):
            r0 = pl.multiple_of((g * sub + u) * tk, tk)
            for hh in range(2):
                scores[u, hh] = _dot(k_ref[0, hh, pl.ds(r0, tk), :], qt_ref[0, hh])
        for u in range(sub):
            for hh in range(2):
                s = scores[u, hh]
                if masked:
                    kk = lax.broadcasted_iota(jnp.int32, (tk, tq), 0) + u * tk
                    qq = lax.broadcasted_iota(jnp.int32, (tk, tq), 1)
                    s = jnp.where(kk <= qq, s, NEG_INF)
                m_old = m_ref[hh]
                m_new = jnp.maximum(m_old, jnp.max(s, axis=0, keepdims=True))
                alpha = jnp.exp2(m_old - m_new)
                p = jnp.exp2(s - m_new)
                acc_ref[hh] = alpha * acc_ref[hh] + _dot(vt_ref[0, hh, g * sub + u], p.astype(BF16))
                m_ref[hh] = m_new

    def body(g, carry):
        group(g, False)
        return carry

    lax.fori_loop(0, qi, body, 0)
    group(qi, True)
    ot = jnp.concatenate(
        [acc_ref[hh, :V_HEAD_DIM, :] / acc_ref[hh, V_HEAD_DIM:V_HEAD_DIM + 1, :] for hh in range(2)],
        axis=0)
    o_ref[0] = ot.T.astype(BF16)


def _attention(qt, k, vt, tq):
    B, H, S, _ = k.shape
    tk = vt.shape[-1]
    return pl.pallas_call(
        functools.partial(_attn_kernel, tq=tq, tk=tk),
        grid=(B, H // 2, S // tq),
        in_specs=[
            pl.BlockSpec((1, 2, LANES, tq), lambda b, p, i: (b, p, 0, i)),
            pl.BlockSpec((1, 2, S, LANES), lambda b, p, i: (b, p, 0, 0)),
            pl.BlockSpec((1, 2, S // tk, LANES, tk), lambda b, p, i: (b, p, 0, 0, 0)),
        ],
        out_specs=pl.BlockSpec((1, tq, LANES), lambda b, p, i: (b, i, p)),
        out_shape=jax.ShapeDtypeStruct((B, S, (H // 2) * LANES), BF16),
        scratch_shapes=[pltpu.VMEM((2, 1, tq), F32), pltpu.VMEM((2, LANES, tq), F32)],
        compiler_params=_params("arbitrary", "arbitrary", "arbitrary"),
        name="mla_attention",
    )(qt, k, vt)


def _gelu_tanh(x):
    return 0.5 * x * (1.0 + jnp.tanh(math.sqrt(2.0 / math.pi) * (x + 0.044715 * (x * x * x))))


def _odd_kernel(x_ref, pos_ref, norm_ref, w_in_ref, conv_w_ref, conv_b_ref, wg_ref, bg_ref,
                lam_ref, w_out_ref, out_ref, ext_ref, a_ref, b_ref, hcar_ref):
    i = pl.program_id(1)
    tm = x_ref.shape[1]
    width = lam_ref.shape[1]
    hd = width // LRU_HEADS
    x = x_ref[0]
    z = _dot(_rms(x, norm_ref[...]).astype(BF16), w_in_ref[...])
    gate = z[:, :width]
    xb = z[:, width:]

    @pl.when(i == 0)
    def _():
        ext_ref[0:CONV_HALO, :] = jnp.zeros((CONV_HALO, width), F32)
        hcar_ref[...] = jnp.zeros(hcar_ref.shape, F32)

    ext_ref[CONV_HALO:CONV_HALO + tm, :] = xb
    xc = conv_b_ref[...] + conv_w_ref[CONV_WIDTH - 1:CONV_WIDTH, :] * xb
    for k in range(CONV_WIDTH - 1):
        off = CONV_HALO - (CONV_WIDTH - 1) + k
        xc = xc + conv_w_ref[k:k + 1, :] * ext_ref[off:off + tm, :]
    ext_ref[0:CONV_HALO, :] = ext_ref[tm:tm + CONV_HALO, :]

    xcb = xc.astype(BF16)
    r_parts, i_parts = [], []
    for h in range(LRU_HEADS):
        g = _dot(xcb[:, h * hd:(h + 1) * hd], wg_ref[h])
        r_parts.append(g[:, :hd])
        i_parts.append(g[:, hd:])
    r = jax.nn.sigmoid(jnp.concatenate(r_parts, axis=-1) + bg_ref[0:1, :])
    ig = jax.nn.sigmoid(jnp.concatenate(i_parts, axis=-1) + bg_ref[1:2, :])
    nlam = -lam_ref[...]
    softplus = jnp.maximum(nlam, 0.0) + jnp.log1p(jnp.exp(-jnp.abs(nlam)))
    log_a = -LRU_C * r * softplus
    a = jnp.exp(log_a)
    mult = jnp.sqrt(jnp.maximum(-jnp.tanh(log_a) * (a * a + 1.0), 0.0))
    reset = pos_ref[0] == 0
    a_ref[...] = jnp.where(reset, 0.0, a)
    b_ref[...] = jnp.where(reset, 1.0, mult) * (ig * xc)

    sub = lax.broadcasted_iota(jnp.int32, (8, width), 0)

    def scan_tile(ci, hprev):
        r0 = pl.multiple_of(ci * 8, 8)
        at = a_ref[pl.ds(r0, 8), :]
        bt = b_ref[pl.ds(r0, 8), :]
        for d in (1, 2, 4):
            keep = sub >= d
            a_s = jnp.where(keep, pltpu.roll(at, d, 0), 1.0)
            b_s = jnp.where(keep, pltpu.roll(bt, d, 0), 0.0)
            bt = at * b_s + bt
            at = at * a_s
        hcur = at * hprev + bt
        b_ref[pl.ds(r0, 8), :] = hcur
        return jnp.broadcast_to(hcur[7:8, :], (8, width))

    hcar_ref[...] = lax.fori_loop(0, tm // 8, scan_tile, hcar_ref[...])
    y = _gelu_tanh(gate) * b_ref[...]
    out_ref[0] = x + _dot(y.astype(BF16), w_out_ref[...])


def _odd_mixer(x, pos_col, norm, w_in, conv_w, conv_b, wg, bg, lam, w_out, tm):
    B, S, D = x.shape
    width = lam.shape[1]
    const = lambda b, i: (0, 0)
    return pl.pallas_call(
        _odd_kernel,
        grid=(B, S // tm),
        in_specs=[
            pl.BlockSpec((1, tm, D), lambda b, i: (b, i, 0)),
            pl.BlockSpec((1, tm, 1), lambda b, i: (b, i, 0)),
            pl.BlockSpec(norm.shape, const),
            pl.BlockSpec(w_in.shape, const),
            pl.BlockSpec(conv_w.shape, const),
            pl.BlockSpec(conv_b.shape, const),
            pl.BlockSpec(wg.shape, lambda b, i: (0, 0, 0)),
            pl.BlockSpec(bg.shape, const),
            pl.BlockSpec(lam.shape, const),
            pl.BlockSpec(w_out.shape, const),
        ],
        out_specs=pl.BlockSpec((1, tm, D), lambda b, i: (b, i, 0)),
        out_shape=jax.ShapeDtypeStruct((B, S, D), F32),
        scratch_shapes=[pltpu.VMEM((CONV_HALO + tm, width), F32),
                        pltpu.VMEM((tm, width), F32), pltpu.VMEM((tm, width), F32),
                        pltpu.VMEM((8, width), F32)],
        compiler_params=_params("arbitrary", "arbitrary"),
        name="odd_mixer",
    )(x, pos_col, norm, w_in, conv_w, conv_b, wg, bg, lam, w_out)


def _memkv_kernel(mem_ref, norm_ref, w_ref, out_ref):
    out_ref[0] = _dot(_rms(mem_ref[0], norm_ref[...]).astype(BF16), w_ref[...]).astype(BF16)


def _memkv(mem, norm, w_kv):
    B, M, D = mem.shape
    return pl.pallas_call(
        _memkv_kernel,
        grid=(B,),
        in_specs=[pl.BlockSpec((1, M, D), lambda b: (b, 0, 0)),
                  pl.BlockSpec(norm.shape, lambda b: (0, 0)),
                  pl.BlockSpec(w_kv.shape, lambda b: (0, 0))],
        out_specs=pl.BlockSpec((1, M, w_kv.shape[1]), lambda b: (b, 0, 0)),
        out_shape=jax.ShapeDtypeStruct((B, M, w_kv.shape[1]), BF16),
        compiler_params=_params("arbitrary"),
        name="mem_kv",
    )(mem, norm, w_kv)


def _xattn_kernel(*refs, with_mix):
    if with_mix:
        x_ref, yp_ref, ya_ref, wop_ref, woa_ref, norm_ref, wq_ref, kv_ref, wo_ref, out_ref = refs
        x = x_ref[...] + _dot(yp_ref[...], wop_ref[...]) + _dot(ya_ref[...], woa_ref[...])
    else:
        x_ref, norm_ref, wq_ref, kv_ref, wo_ref, out_ref = refs
        x = x_ref[...]
    d = x.shape[1]
    hd = d // MEM_HEADS
    q = _dot(_rms(x, norm_ref[...]).astype(BF16), wq_ref[...]).astype(BF16)
    outs = []
    for h in range(MEM_HEADS):
        kh = kv_ref[0, :, h * hd:(h + 1) * hd]
        vh = kv_ref[0, :, d + h * hd:d + (h + 1) * hd]
        s = _dot_nt(q[:, h * hd:(h + 1) * hd], kh) * (hd ** -0.5)
        p = jnp.exp(s - jnp.max(s, axis=-1, keepdims=True))
        o = _dot(p.astype(BF16), vh) / jnp.sum(p, axis=-1, keepdims=True)
        outs.append(o.astype(BF16))
    out_ref[...] = x + _dot(jnp.concatenate(outs, axis=-1), wo_ref[...])


def _xattn(x2d, mix, norm, wq, kv, wo, tm, tiles_per_batch):
    T, D = x2d.shape
    const = lambda i: (0, 0)
    row = lambda i: (i, 0)
    args = [x2d]
    specs = [pl.BlockSpec((tm, D), row)]
    if mix is not None:
        yp, ya, wop, woa = mix
        args += [yp, ya, wop, woa]
        specs += [pl.BlockSpec((tm, yp.shape[1]), row), pl.BlockSpec((tm, ya.shape[1]), row),
                  pl.BlockSpec(wop.shape, const), pl.BlockSpec(woa.shape, const)]
    args += [norm, wq, kv, wo]
    specs += [pl.BlockSpec(norm.shape, const), pl.BlockSpec(wq.shape, const),
              pl.BlockSpec((1,) + kv.shape[1:], lambda i: (i // tiles_per_batch, 0, 0)),
              pl.BlockSpec(wo.shape, const)]
    return pl.pallas_call(
        functools.partial(_xattn_kernel, with_mix=mix is not None),
        grid=(T // tm,),
        in_specs=specs,
        out_specs=pl.BlockSpec((tm, D), row),
        out_shape=jax.ShapeDtypeStruct((T, D), F32),
        compiler_params=_params("arbitrary"),
        name="mem_xattn_mix" if mix is not None else "mem_xattn",
    )(*args)


def _ffn_kernel(x_ref, norm_ref, wg_ref, wu_ref, wd_ref, fnorm_ref, out_ref, h_ref, acc_ref,
                *, final):
    f = pl.program_id(1)

    @pl.when(f == 0)
    def _():
        x = x_ref[...]
        h_ref[...] = _rms(x, norm_ref[...]).astype(BF16)
        acc_ref[...] = x

    h = h_ref[...]
    g = _dot(h, wg_ref[...])
    u = _dot(h, wu_ref[...])
    act = (g * jax.nn.sigmoid(g) * u).astype(BF16)
    acc_ref[...] += _dot(act, wd_ref[...])

    @pl.when(f == pl.num_programs(1) - 1)
    def _():
        y = acc_ref[...]
        out_ref[...] = _rms(y, fnorm_ref[...]) if final else y


def _ffn(x2d, norm, w_gate_up, w_down, fnorm, tm, tf, final):
    T, D = x2d.shape
    ff = w_down.shape[0]
    nf = ff // tf
    return pl.pallas_call(
        functools.partial(_ffn_kernel, final=final),
        grid=(T // tm, nf),
        in_specs=[
            pl.BlockSpec((tm, D), lambda i, f: (i, 0)),
            pl.BlockSpec(norm.shape, lambda i, f: (0, 0)),
            pl.BlockSpec((D, tf), lambda i, f: (0, f)),
            pl.BlockSpec((D, tf), lambda i, f: (0, f + nf)),
            pl.BlockSpec((tf, D), lambda i, f: (f, 0)),
            pl.BlockSpec(fnorm.shape, lambda i, f: (0, 0)),
        ],
        out_specs=pl.BlockSpec((tm, D), lambda i, f: (i, 0)),
        out_shape=jax.ShapeDtypeStruct((T, D), F32),
        scratch_shapes=[pltpu.VMEM((tm, D), BF16), pltpu.VMEM((tm, D), F32)],
        compiler_params=_params("arbitrary", "arbitrary"),
        name="ffn_final" if final else "ffn",
    )(x2d, norm, w_gate_up, w_gate_up, w_down, fnorm)


def _tiles(S, ff):
    tm = min(512, S)
    tq = tm
    tk = min(256, tq)
    tf = ff
    for n in range(2, ff // LANES + 1):
        if ff % (n * LANES) == 0:
            tf = ff // n
            break
    return tm, tq, tk, tf


def _pad_cols(w, groups, width):
    k = w.shape[0]
    w = w.reshape(k, groups, -1)
    return jnp.pad(w, ((0, 0), (0, 0), (0, width - w.shape[2]))).reshape(k, groups * width)


def kernel(x, mem, positions, ev_norm, ev_w_in, ev_pool_w, ev_pool_scale, ev_q_norm, ev_w_q_up, ev_kv_norm, ev_w_kv_up, ev_w_out, od_norm, od_w_in, od_conv_w, od_conv_b, od_w_rgate, od_b_rgate, od_w_igate, od_b_igate, od_lambda, od_w_out, xa_norm_x, xa_norm_mem, xa_w_q, xa_w_kv, xa_w_o, ffn_norm, ffn_w_gate_up, ffn_w_down, final_norm):
    B, S, D = x.shape
    depth = xa_w_q.shape[0]
    ff = ffn_w_down.shape[1]
    tm, tq, tk, tf = _tiles(S, ff)
    pos_col = positions.reshape(B, S, 1)
    pool_dim = ev_pool_scale.shape[1]
    q_rank = ev_q_norm.shape[1]
    kv_rank = ev_kv_norm.shape[1]

    half = QK_ROPE_DIM // 2
    inv_freq = ROPE_BASE ** (-jnp.arange(0, QK_ROPE_DIM, 2, dtype=F32) / QK_ROPE_DIM)
    lo = slice(QK_NOPE_DIM, QK_NOPE_DIM + half)
    hi = slice(QK_NOPE_DIM + half, QK_NOPE_DIM + QK_ROPE_DIM)
    rope_tab = jnp.zeros((8, LANES), F32)
    rope_tab = rope_tab.at[0, lo].set(inv_freq).at[0, hi].set(inv_freq)
    rope_tab = rope_tab.at[1, lo].set(-1.0).at[2, hi].set(1.0).at[3, V_HEAD_DIM].set(1.0)

    row = lambda v: v.reshape(1, -1)
    h = x
    for layer in range(depth):
        j = layer // 2
        if layer % 2 == 0:
            lat = pool_dim + q_rank + kv_rank
            w_in = jnp.concatenate(
                [ev_w_in[j][:, :lat], jnp.zeros((D, QK_NOPE_DIM), F32), ev_w_in[j][:, lat:],
                 jnp.zeros((D, LANES - QK_DIM), F32)], axis=1).astype(BF16)
            wq = _pad_cols(ev_w_q_up[j], MLA_HEADS, LANES).astype(BF16)
            wkv = ev_w_kv_up[j].reshape(kv_rank, MLA_HEADS, QK_NOPE_DIM + V_HEAD_DIM)
            wk = _pad_cols(wkv[:, :, :QK_NOPE_DIM].reshape(kv_rank, -1), MLA_HEADS, LANES).astype(BF16)
            wv = _pad_cols(wkv[:, :, QK_NOPE_DIM:].reshape(kv_rank, -1), MLA_HEADS, LANES).astype(BF16)
            y_pool, q, k, v = _even_front(
                h, pos_col, row(ev_norm[j]), w_in, ev_pool_w[j].astype(BF16), row(ev_pool_scale[j]),
                row(ev_q_norm[j]), wq, row(ev_kv_norm[j]), wk, wv, rope_tab, tm, tk)
            y_att = _attention(q, k, v, tq)
            w_out = ev_w_out[j].astype(BF16)
            mix = (y_pool.reshape(B * S, -1), y_att.reshape(B * S, -1),
                   w_out[:pool_dim], w_out[pool_dim:])
            h2d = h.reshape(B * S, D)
        else:
            lw = od_lambda.shape[1]
            hd = lw // LRU_HEADS
            wg = jnp.concatenate([od_w_rgate[j], od_w_igate[j]], axis=-1).astype(BF16)
            bg = jnp.stack([od_b_rgate[j], od_b_igate[j]])
            h = _odd_mixer(h, pos_col, row(od_norm[j]), od_w_in[j].astype(BF16), od_conv_w[j],
                           row(od_conv_b[j]), wg, bg, row(od_lambda[j]), od_w_out[j].astype(BF16), tm)
            mix = None
            h2d = h.reshape(B * S, D)
        kv = _memkv(mem, row(xa_norm_mem[layer]), xa_w_kv[layer].astype(BF16))
        h2d = _xattn(h2d, mix, row(xa_norm_x[layer]), xa_w_q[layer].astype(BF16), kv,
                     xa_w_o[layer].astype(BF16), tm, S // tm)
        h2d = _ffn(h2d, row(ffn_norm[layer]), ffn_w_gate_up[layer].astype(BF16),
                   ffn_w_down[layer].astype(BF16), row(final_norm), tm, tf,
                   final=layer == depth - 1)
        h = h2d.reshape(B, S, D)
    return h
```

```python
import functools
import math

import jax
import jax.numpy as jnp
from jax import lax
from jax.experimental import pallas as pl
from jax.experimental.pallas import tpu as pltpu

F32 = jnp.float32
BF16 = jnp.bfloat16

LANES = 128

POOL_WINDOWS = (2, 4, 8, 16)
POOL_HALO = 16
MLA_HEADS = 8
QK_NOPE_DIM = 64
QK_ROPE_DIM = 32
QK_DIM = QK_NOPE_DIM + QK_ROPE_DIM
V_HEAD_DIM = 64
ATTN_HEADS_PER_STEP = 4
QK_EXP2_SCALE = (QK_DIM ** -0.5) * math.log2(math.e)
ROPE_BASE = 10000.0
LRU_HEADS = 4
CONV_WIDTH = 4
CONV_HALO = 8
LRU_C = 8.0
MEM_HEADS = 4
RMS_EPS = 1e-6
NEG_INF = -1e30

VMEM_LIMIT_BYTES = 56 * 1024 * 1024


def _params(*sem):
    return pltpu.CompilerParams(dimension_semantics=sem, vmem_limit_bytes=VMEM_LIMIT_BYTES)


def _rms(x, g):
    return x * lax.rsqrt(jnp.mean(x * x, axis=-1, keepdims=True) + RMS_EPS) * g


def _dot(a, b):
    return jnp.dot(a, b, preferred_element_type=F32)


def _dot_nt(a, b):
    return lax.dot_general(a, b, (((1,), (1,)), ((), ())), preferred_element_type=F32)


def _even_front_kernel(x_ref, pos_ref, norm_ref, w_in_ref, pool_w_ref, pool_scale_ref,
                       qn_ref, wq_ref, kvn_ref, wk_ref, wv_ref, freq_ref, ones_ref,
                       ypool_ref, q_ref, k_ref, v_ref, ext_ref):
    i = pl.program_id(1)
    tm = x_ref.shape[1]
    pool_dim = ypool_ref.shape[2]
    h = _rms(x_ref[0], norm_ref[...]).astype(BF16)
    z = _dot(h, w_in_ref[...])

    @pl.when(i == 0)
    def _():
        ext_ref[0:POOL_HALO, :] = jnp.zeros((POOL_HALO, pool_dim), F32)

    u = z[:, :pool_dim]
    ext_ref[POOL_HALO:POOL_HALO + tm, :] = u
    t = i * tm + lax.broadcasted_iota(jnp.int32, (tm, 1), 0)
    parts = []
    for g, w in enumerate(POOL_WINDOWS):
        cols = slice(g * LANES, (g + 1) * LANES)
        ug = u[:, cols]
        acc = ext_ref[:, cols]
        d = 1
        while d < w:
            acc = acc + pltpu.roll(acc, d, 0)
            d *= 2
        acc = acc[POOL_HALO:, :]
        cnt = jnp.minimum(t + 1, w).astype(F32)
        pooled = acc / cnt - ug
        parts.append(_dot(pooled.astype(BF16), pool_w_ref[g]))
    y_pool = jnp.concatenate(parts, axis=-1) * pool_scale_ref[...]
    ypool_ref[0] = y_pool.astype(BF16)
    ext_ref[0:POOL_HALO, :] = ext_ref[tm:tm + POOL_HALO, :]

    ang = freq_ref[...] * pos_ref[0].astype(F32)
    cos = jnp.cos(ang)
    sin = jnp.sin(ang)
    half = QK_ROPE_DIM // 2
    r1, r2, r3 = QK_NOPE_DIM, QK_NOPE_DIM + half, QK_DIM

    def rope_t(xt):
        t1, t2 = xt[r1:r2, :], xt[r2:r3, :]
        return jnp.concatenate(
            [xt[:r1, :], t1 * cos - t2 * sin, t2 * cos + t1 * sin, xt[r3:, :]], axis=0)

    q_lat = z[:, pool_dim:pool_dim + qn_ref.shape[1]]
    qf = _dot(_rms(q_lat, qn_ref[...]).astype(BF16), wq_ref[...])
    for hd in range(MLA_HEADS):
        qt = rope_t(qf[:, hd * LANES:(hd + 1) * LANES].T)
        q_ref[0, hd] = (qt * QK_EXP2_SCALE).astype(BF16)

    kv0 = pool_dim + qn_ref.shape[1]
    kv_lat = _rms(z[:, kv0:kv0 + kvn_ref.shape[1]], kvn_ref[...]).astype(BF16)
    kf = _dot(kv_lat, wk_ref[...])
    vf = _dot(kv_lat, wv_ref[...])
    k_rope = rope_t(z[:, kv0 + kvn_ref.shape[1]:].T).T
    for hd in range(MLA_HEADS):
        k_ref[0, hd] = (kf[:, hd * LANES:(hd + 1) * LANES] + k_rope).astype(BF16)
    ones_col = ones_ref[...]
    tk = v_ref.shape[4]
    for hd in range(MLA_HEADS):
        vt = (vf[:, hd * LANES:(hd + 1) * LANES] + ones_col).T.astype(BF16)
        for u in range(tm // tk):
            v_ref[0, hd, u] = vt[:, u * tk:(u + 1) * tk]


def _even_front(x, pos_row, norm, w_in, pool_w, pool_scale, qn, wq, kvn, wk, wv, freq_col, ones_col,
                tm, tk):
    B, S, D = x.shape
    pool_dim = pool_scale.shape[1]
    const = lambda b, i: (0, 0)
    return pl.pallas_call(
        _even_front_kernel,
        grid=(B, S // tm),
        in_specs=[
            pl.BlockSpec((1, tm, D), lambda b, i: (b, i, 0)),
            pl.BlockSpec((1, 1, tm), lambda b, i: (b, 0, i)),
            pl.BlockSpec(norm.shape, const),
            pl.BlockSpec(w_in.shape, const),
            pl.BlockSpec(pool_w.shape, lambda b, i: (0, 0, 0)),
            pl.BlockSpec(pool_scale.shape, const),
            pl.BlockSpec(qn.shape, const),
            pl.BlockSpec(wq.shape, const),
            pl.BlockSpec(kvn.shape, const),
            pl.BlockSpec(wk.shape, const),
            pl.BlockSpec(wv.shape, const),
            pl.BlockSpec(freq_col.shape, const),
            pl.BlockSpec(ones_col.shape, const),
        ],
        out_specs=[
            pl.BlockSpec((1, tm, pool_dim), lambda b, i: (b, i, 0)),
            pl.BlockSpec((1, MLA_HEADS, LANES, tm), lambda b, i: (b, 0, 0, i)),
            pl.BlockSpec((1, MLA_HEADS, tm, LANES), lambda b, i: (b, 0, i, 0)),
            pl.BlockSpec((1, MLA_HEADS, tm // tk, LANES, tk), lambda b, i: (b, 0, i, 0, 0)),
        ],
        out_shape=[
            jax.ShapeDtypeStruct((B, S, pool_dim), BF16),
            jax.ShapeDtypeStruct((B, MLA_HEADS, LANES, S), BF16),
            jax.ShapeDtypeStruct((B, MLA_HEADS, S, LANES), BF16),
            jax.ShapeDtypeStruct((B, MLA_HEADS, S // tk, LANES, tk), BF16),
        ],
        scratch_shapes=[pltpu.VMEM((POOL_HALO + tm, pool_dim), F32)],
        compiler_params=_params("arbitrary", "arbitrary"),
        name="even_front",
    )(x, pos_row, norm, w_in, pool_w, pool_scale, qn, wq, kvn, wk, wv, freq_col, ones_col)


def _attn_kernel(qt_ref, k_ref, vt_ref, o_ref, m_ref, mt_ref, acc_ref, s_ref, *, tq, tk, hps):
    qi = pl.program_id(2)
    sub = tq // tk
    m_ref[...] = jnp.full(m_ref.shape, NEG_INF, F32)
    acc_ref[...] = jnp.zeros(acc_ref.shape, F32)

    def stage(g_fin, g_new, masked):
        for hh in range(hps):
            if g_fin is not None:
                m_old = m_ref[hh]
                m_new = jnp.maximum(m_old, mt_ref[hh])
                alpha = jnp.exp2(m_old - m_new)
            pv = None
            mt = None
            for u in range(sub):
                if g_fin is not None:
                    p = jnp.exp2(s_ref[u, hh] - m_new).astype(BF16)
                    d = _dot(vt_ref[0, hh, g_fin * sub + u], p)
                    pv = d if pv is None else pv + d
                if g_new is not None:
                    r0 = pl.multiple_of(g_new * tq + u * tk, tk)
                    s = _dot(k_ref[0, hh, pl.ds(r0, tk), :], qt_ref[0, hh])
                    if masked:
                        kk = lax.broadcasted_iota(jnp.int32, (tk, tq), 0) + u * tk
                        qq = lax.broadcasted_iota(jnp.int32, (tk, tq), 1)
                        s = jnp.where(kk <= qq, s, NEG_INF)
                    s_ref[u, hh] = s
                    cm = jnp.max(s, axis=0, keepdims=True)
                    mt = cm if mt is None else jnp.maximum(mt, cm)
            if g_fin is not None:
                acc_ref[hh] = alpha * acc_ref[hh] + pv
                m_ref[hh] = m_new
            if g_new is not None:
                mt_ref[hh] = mt

    @pl.when(qi == 0)
    def _():
        stage(None, 0, True)

    @pl.when(qi > 0)
    def _():
        stage(None, 0, False)

    def body(g, carry):
        stage(g, g + 1, False)
        return carry

    lax.fori_loop(0, qi - 1, body, 0)

    @pl.when(qi > 0)
    def _():
        stage(qi - 1, qi, True)

    stage(qi, None, False)
    for pr in range(hps // 2):
        ot = jnp.concatenate(
            [acc_ref[hh, :V_HEAD_DIM, :] / acc_ref[hh, V_HEAD_DIM:V_HEAD_DIM + 1, :]
             for hh in (2 * pr, 2 * pr + 1)], axis=0)
        o_ref[0, :, pr * LANES:(pr + 1) * LANES] = ot.T.astype(BF16)


def _attention(qt, k, vt, tq, hps):
    B, H, S, _ = k.shape
    tk = vt.shape[-1]
    resident = pl.Buffered(1)
    return pl.pallas_call(
        functools.partial(_attn_kernel, tq=tq, tk=tk, hps=hps),
        grid=(B, H // hps, S // tq),
        in_specs=[
            pl.BlockSpec((1, hps, LANES, tq), lambda b, p, i: (b, p, 0, i)),
            pl.BlockSpec((1, hps, S, LANES), lambda b, p, i: (b, p, 0, 0), pipeline_mode=resident),
            pl.BlockSpec((1, hps, S // tk, LANES, tk), lambda b, p, i: (b, p, 0, 0, 0),
                         pipeline_mode=resident),
        ],
        out_specs=pl.BlockSpec((1, tq, (hps // 2) * LANES), lambda b, p, i: (b, i, p)),
        out_shape=jax.ShapeDtypeStruct((B, S, (H // 2) * LANES), BF16),
        scratch_shapes=[pltpu.VMEM((hps, 1, tq), F32), pltpu.VMEM((hps, 1, tq), F32),
                        pltpu.VMEM((hps, LANES, tq), F32), pltpu.VMEM((tq // tk, hps, tk, tq), F32)],
        compiler_params=_params("arbitrary", "arbitrary", "arbitrary"),
        name="mla_attention",
    )(qt, k, vt)


def _gelu_tanh(x):
    return 0.5 * x * (1.0 + jnp.tanh(math.sqrt(2.0 / math.pi) * (x + 0.044715 * (x * x * x))))


def _sigmoid(x):
    return 0.5 * jnp.tanh(0.5 * x) + 0.5


def _odd_kernel(x_ref, pos_ref, norm_ref, w_in_ref, conv_w_ref, conv_b_ref, wg_ref, bg_ref,
                lam_ref, w_out_ref, out_ref, ext_ref, a_ref, b_ref, hcar_ref):
    i = pl.program_id(1)
    tm = x_ref.shape[1]
    width = lam_ref.shape[1]
    hd = width // LRU_HEADS
    x = x_ref[0]
    z = _dot(_rms(x, norm_ref[...]).astype(BF16), w_in_ref[...])
    gate = z[:, :width]
    xb = z[:, width:]

    @pl.when(i == 0)
    def _():
        ext_ref[0:CONV_HALO, :] = jnp.zeros((CONV_HALO, width), F32)
        hcar_ref[...] = jnp.zeros(hcar_ref.shape, F32)

    ext_ref[CONV_HALO:CONV_HALO + tm, :] = xb
    xc = conv_b_ref[...] + conv_w_ref[CONV_WIDTH - 1:CONV_WIDTH, :] * xb
    for k in range(CONV_WIDTH - 1):
        off = CONV_HALO - (CONV_WIDTH - 1) + k
        xc = xc + conv_w_ref[k:k + 1, :] * ext_ref[off:off + tm, :]
    ext_ref[0:CONV_HALO, :] = ext_ref[tm:tm + CONV_HALO, :]

    xcb = xc.astype(BF16)
    r_parts, i_parts = [], []
    for h in range(LRU_HEADS):
        g = _dot(xcb[:, h * hd:(h + 1) * hd], wg_ref[h])
        r_parts.append(g[:, :hd])
        i_parts.append(g[:, hd:])
    r = _sigmoid(jnp.concatenate(r_parts, axis=-1) + bg_ref[0:1, :])
    ig = _sigmoid(jnp.concatenate(i_parts, axis=-1) + bg_ref[1:2, :])
    nlam = -lam_ref[...]
    softplus = jnp.maximum(nlam, 0.0) + jnp.log1p(jnp.exp(-jnp.abs(nlam)))
    log_a = -LRU_C * r * softplus
    a = jnp.exp(log_a)
    mult = jnp.sqrt(jnp.maximum(-jnp.tanh(log_a) * (a * a + 1.0), 0.0))
    reset = pos_ref[0] == 0
    a_ref[...] = jnp.where(reset, 0.0, a)
    b_ref[...] = jnp.where(reset, 1.0, mult) * (ig * xc)

    sub = lax.broadcasted_iota(jnp.int32, (8, width), 0)

    def scan_tile(ci, hprev):
        r0 = pl.multiple_of(ci * 8, 8)
        at = a_ref[pl.ds(r0, 8), :]
        bt = b_ref[pl.ds(r0, 8), :]
        for d in (1, 2, 4):
            keep = sub >= d
            a_s = jnp.where(keep, pltpu.roll(at, d, 0), 1.0)
            b_s = jnp.where(keep, pltpu.roll(bt, d, 0), 0.0)
            bt = at * b_s + bt
            at = at * a_s
        hcur = at * hprev + bt
        b_ref[pl.ds(r0, 8), :] = hcur
        return jnp.broadcast_to(hcur[7:8, :], (8, width))

    hcar_ref[...] = lax.fori_loop(0, tm // 8, scan_tile, hcar_ref[...])
    y = _gelu_tanh(gate) * b_ref[...]
    out_ref[0] = x + _dot(y.astype(BF16), w_out_ref[...])


def _odd_mixer(x, pos_col, norm, w_in, conv_w, conv_b, wg, bg, lam, w_out, tm):
    B, S, D = x.shape
    width = lam.shape[1]
    const = lambda b, i: (0, 0)
    return pl.pallas_call(
        _odd_kernel,
        grid=(B, S // tm),
        in_specs=[
            pl.BlockSpec((1, tm, D), lambda b, i: (b, i, 0)),
            pl.BlockSpec((1, tm, 1), lambda b, i: (b, i, 0)),
            pl.BlockSpec(norm.shape, const),
            pl.BlockSpec(w_in.shape, const),
            pl.BlockSpec(conv_w.shape, const),
            pl.BlockSpec(conv_b.shape, const),
            pl.BlockSpec(wg.shape, lambda b, i: (0, 0, 0)),
            pl.BlockSpec(bg.shape, const),
            pl.BlockSpec(lam.shape, const),
            pl.BlockSpec(w_out.shape, const),
        ],
        out_specs=pl.BlockSpec((1, tm, D), lambda b, i: (b, i, 0)),
        out_shape=jax.ShapeDtypeStruct((B, S, D), F32),
        scratch_shapes=[pltpu.VMEM((CONV_HALO + tm, width), F32),
                        pltpu.VMEM((tm, width), F32), pltpu.VMEM((tm, width), F32),
                        pltpu.VMEM((8, width), F32)],
        compiler_params=_params("arbitrary", "arbitrary"),
        name="odd_mixer",
    )(x, pos_col, norm, w_in, conv_w, conv_b, wg, bg, lam, w_out)


def _memkv_kernel(mem_ref, norm_ref, w_ref, out_ref):
    out_ref[0] = _dot(_rms(mem_ref[0], norm_ref[...]).astype(BF16), w_ref[...]).astype(BF16)


def _memkv(mem, norm, w_kv):
    B, M, D = mem.shape
    return pl.pallas_call(
        _memkv_kernel,
        grid=(B,),
        in_specs=[pl.BlockSpec((1, M, D), lambda b: (b, 0, 0)),
                  pl.BlockSpec(norm.shape, lambda b: (0, 0)),
                  pl.BlockSpec(w_kv.shape, lambda b: (0, 0))],
        out_specs=pl.BlockSpec((1, M, w_kv.shape[1]), lambda b: (b, 0, 0)),
        out_shape=jax.ShapeDtypeStruct((B, M, w_kv.shape[1]), BF16),
        compiler_params=_params("arbitrary"),
        name="mem_kv",
    )(mem, norm, w_kv)


def _xattn_kernel(*refs, with_mix):
    if with_mix:
        x_ref, yp_ref, ya_ref, wop_ref, woa_ref, norm_ref, wq_ref, kv_ref, wo_ref, out_ref = refs
        x = x_ref[...] + _dot(yp_ref[...], wop_ref[...]) + _dot(ya_ref[...], woa_ref[...])
    else:
        x_ref, norm_ref, wq_ref, kv_ref, wo_ref, out_ref = refs
        x = x_ref[...]
    d = x.shape[1]
    hd = d // MEM_HEADS
    q = _dot(_rms(x, norm_ref[...]).astype(BF16), wq_ref[...]).astype(BF16)
    outs = []
    for h in range(MEM_HEADS):
        kh = kv_ref[0, :, h * hd:(h + 1) * hd]
        vh = kv_ref[0, :, d + h * hd:d + (h + 1) * hd]
        s = _dot_nt(q[:, h * hd:(h + 1) * hd], kh) * (hd ** -0.5)
        p = jnp.exp(s - jnp.max(s, axis=-1, keepdims=True))
        o = _dot(p.astype(BF16), vh) / jnp.sum(p, axis=-1, keepdims=True)
        outs.append(o.astype(BF16))
    out_ref[...] = x + _dot(jnp.concatenate(outs, axis=-1), wo_ref[...])


def _xattn(x2d, mix, norm, wq, kv, wo, tm, tiles_per_batch):
    T, D = x2d.shape
    const = lambda i: (0, 0)
    row = lambda i: (i, 0)
    args = [x2d]
    specs = [pl.BlockSpec((tm, D), row)]
    if mix is not None:
        yp, ya, wop, woa = mix
        args += [yp, ya, wop, woa]
        specs += [pl.BlockSpec((tm, yp.shape[1]), row), pl.BlockSpec((tm, ya.shape[1]), row),
                  pl.BlockSpec(wop.shape, const), pl.BlockSpec(woa.shape, const)]
    args += [norm, wq, kv, wo]
    specs += [pl.BlockSpec(norm.shape, const), pl.BlockSpec(wq.shape, const),
              pl.BlockSpec((1,) + kv.shape[1:], lambda i: (i // tiles_per_batch, 0, 0)),
              pl.BlockSpec(wo.shape, const)]
    return pl.pallas_call(
        functools.partial(_xattn_kernel, with_mix=mix is not None),
        grid=(T // tm,),
        in_specs=specs,
        out_specs=pl.BlockSpec((tm, D), row),
        out_shape=jax.ShapeDtypeStruct((T, D), F32),
        compiler_params=_params("arbitrary"),
        name="mem_xattn_mix" if mix is not None else "mem_xattn",
    )(*args)


def _ffn_kernel(x_ref, norm_ref, wg_ref, wu_ref, wd_ref, fnorm_ref, out_ref, h_ref, acc_ref,
                *, final):
    f = pl.program_id(1)

    @pl.when(f == 0)
    def _():
        x = x_ref[...]
        h_ref[...] = _rms(x, norm_ref[...]).astype(BF16)
        acc_ref[...] = x

    h = h_ref[...]
    g = _dot(h, wg_ref[...])
    u = _dot(h, wu_ref[...])
    act = (g * jax.nn.sigmoid(g) * u).astype(BF16)
    acc_ref[...] += _dot(act, wd_ref[...])

    @pl.when(f == pl.num_programs(1) - 1)
    def _():
        y = acc_ref[...]
        out_ref[...] = _rms(y, fnorm_ref[...]) if final else y


def _ffn(x2d, norm, w_gate_up, w_down, fnorm, tm, tf, final):
    T, D = x2d.shape
    ff = w_down.shape[0]
    nf = ff // tf
    return pl.pallas_call(
        functools.partial(_ffn_kernel, final=final),
        grid=(T // tm, nf),
        in_specs=[
            pl.BlockSpec((tm, D), lambda i, f: (i, 0)),
            pl.BlockSpec(norm.shape, lambda i, f: (0, 0)),
            pl.BlockSpec((D, tf), lambda i, f: (0, f)),
            pl.BlockSpec((D, tf), lambda i, f: (0, f + nf)),
            pl.BlockSpec((tf, D), lambda i, f: (f, 0)),
            pl.BlockSpec(fnorm.shape, lambda i, f: (0, 0)),
        ],
        out_specs=pl.BlockSpec((tm, D), lambda i, f: (i, 0)),
        out_shape=jax.ShapeDtypeStruct((T, D), F32),
        scratch_shapes=[pltpu.VMEM((tm, D), BF16), pltpu.VMEM((tm, D), F32)],
        compiler_params=_params("arbitrary", "arbitrary"),
        name="ffn_final" if final else "ffn",
    )(x2d, norm, w_gate_up, w_gate_up, w_down, fnorm)


def _tiles(S, ff):
    tm = min(512, S)
    tq = tm
    tk = min(256, tq)
    tf = ff
    for n in range(2, ff // LANES + 1):
        if ff % (n * LANES) == 0:
            tf = ff // n
            break
    return tm, tq, tk, tf


def _pad_cols(w, groups, width):
    k = w.shape[0]
    w = w.reshape(k, groups, -1)
    return jnp.pad(w, ((0, 0), (0, 0), (0, width - w.shape[2]))).reshape(k, groups * width)


def kernel(x, mem, positions, ev_norm, ev_w_in, ev_pool_w, ev_pool_scale, ev_q_norm, ev_w_q_up, ev_kv_norm, ev_w_kv_up, ev_w_out, od_norm, od_w_in, od_conv_w, od_conv_b, od_w_rgate, od_b_rgate, od_w_igate, od_b_igate, od_lambda, od_w_out, xa_norm_x, xa_norm_mem, xa_w_q, xa_w_kv, xa_w_o, ffn_norm, ffn_w_gate_up, ffn_w_down, final_norm):
    B, S, D = x.shape
    depth = xa_w_q.shape[0]
    ff = ffn_w_down.shape[1]
    tm, tq, tk, tf = _tiles(S, ff)
    pos_col = positions.reshape(B, S, 1)
    pool_dim = ev_pool_scale.shape[1]
    q_rank = ev_q_norm.shape[1]
    kv_rank = ev_kv_norm.shape[1]

    pos_row = positions.reshape(B, 1, S)
    inv_freq = ROPE_BASE ** (-jnp.arange(0, QK_ROPE_DIM, 2, dtype=F32) / QK_ROPE_DIM)
    freq_col = inv_freq.reshape(-1, 1)
    ones_col = jnp.zeros((1, LANES), F32).at[0, V_HEAD_DIM].set(1.0)

    row = lambda v: v.reshape(1, -1)
    h = x
    for layer in range(depth):
        j = layer // 2
        if layer % 2 == 0:
            lat = pool_dim + q_rank + kv_rank
            w_in = jnp.concatenate(
                [ev_w_in[j][:, :lat], jnp.zeros((D, QK_NOPE_DIM), F32), ev_w_in[j][:, lat:],
                 jnp.zeros((D, LANES - QK_DIM), F32)], axis=1).astype(BF16)
            wq = _pad_cols(ev_w_q_up[j], MLA_HEADS, LANES).astype(BF16)
            wkv = ev_w_kv_up[j].reshape(kv_rank, MLA_HEADS, QK_NOPE_DIM + V_HEAD_DIM)
            wk = _pad_cols(wkv[:, :, :QK_NOPE_DIM].reshape(kv_rank, -1), MLA_HEADS, LANES).astype(BF16)
            wv = _pad_cols(wkv[:, :, QK_NOPE_DIM:].reshape(kv_rank, -1), MLA_HEADS, LANES).astype(BF16)
            y_pool, q, k, v = _even_front(
                h, pos_row, row(ev_norm[j]), w_in, ev_pool_w[j].astype(BF16), row(ev_pool_scale[j]),
                row(ev_q_norm[j]), wq, row(ev_kv_norm[j]), wk, wv, freq_col, ones_col, tm, tk)
            y_att = _attention(q, k, v, tq, ATTN_HEADS_PER_STEP)
            w_out = ev_w_out[j].astype(BF16)
            mix = (y_pool.reshape(B * S, -1), y_att.reshape(B * S, -1),
                   w_out[:pool_dim], w_out[pool_dim:])
            h2d = h.reshape(B * S, D)
        else:
            lw = od_lambda.shape[1]
            hd = lw // LRU_HEADS
            wg = jnp.concatenate([od_w_rgate[j], od_w_igate[j]], axis=-1).astype(BF16)
            bg = jnp.stack([od_b_rgate[j], od_b_igate[j]])
            h = _odd_mixer(h, pos_col, row(od_norm[j]), od_w_in[j].astype(BF16), od_conv_w[j],
                           row(od_conv_b[j]), wg, bg, row(od_lambda[j]), od_w_out[j].astype(BF16), tm)
            mix = None
            h2d = h.reshape(B * S, D)
        kv = _memkv(mem, row(xa_norm_mem[layer]), xa_w_kv[layer].astype(BF16))
        h2d = _xattn(h2d, mix, row(xa_norm_x[layer]), xa_w_q[layer].astype(BF16), kv,
                     xa_w_o[layer].astype(BF16), tm, S // tm)
        h2d = _ffn(h2d, row(ffn_norm[layer]), ffn_w_gate_up[layer].astype(BF16),
                   ffn_w_down[layer].astype(BF16), row(final_norm), tm, tf,
                   final=layer == depth - 1)
        h = h2d.reshape(B, S, D)
    return h
```

```python
import functools
import math

import jax
import jax.numpy as jnp
from jax import lax
from jax.experimental import pallas as pl
from jax.experimental.pallas import tpu as pltpu

F32 = jnp.float32
BF16 = jnp.bfloat16

LANES = 128

POOL_WINDOWS = (2, 4, 8, 16)
POOL_HALO = 16
MLA_HEADS = 8
QK_NOPE_DIM = 64
QK_ROPE_DIM = 32
QK_DIM = QK_NOPE_DIM + QK_ROPE_DIM
V_HEAD_DIM = 64
ATTN_HEADS_PER_STEP = 4
QK_EXP2_SCALE = (QK_DIM ** -0.5) * math.log2(math.e)
ROPE_BASE = 10000.0
LRU_HEADS = 4
CONV_WIDTH = 4
CONV_HALO = 8
ODD_ROW_BLOCKS = 2
LRU_C = 8.0
MEM_HEADS = 4
RMS_EPS = 1e-6
NEG_INF = -1e30

VMEM_LIMIT_BYTES = 56 * 1024 * 1024


def _params(*sem):
    return pltpu.CompilerParams(dimension_semantics=sem, vmem_limit_bytes=VMEM_LIMIT_BYTES)


def _rms(x, g):
    return x * lax.rsqrt(jnp.mean(x * x, axis=-1, keepdims=True) + RMS_EPS) * g


def _dot(a, b):
    return jnp.dot(a, b, preferred_element_type=F32)


def _dot_nt(a, b):
    return lax.dot_general(a, b, (((1,), (1,)), ((), ())), preferred_element_type=F32)


def _even_front_kernel(x_ref, pos_ref, norm_ref, w_in_ref, pool_w_ref, pool_scale_ref,
                       qn_ref, wq_ref, kvn_ref, wk_ref, wv_ref, freq_ref, ones_ref,
                       ypool_ref, q_ref, k_ref, v_ref, ext_ref):
    i = pl.program_id(1)
    tm = x_ref.shape[1]
    pool_dim = ypool_ref.shape[2]
    h = _rms(x_ref[0], norm_ref[...]).astype(BF16)
    z = _dot(h, w_in_ref[...])

    @pl.when(i == 0)
    def _():
        ext_ref[0:POOL_HALO, :] = jnp.zeros((POOL_HALO, pool_dim), F32)

    u = z[:, :pool_dim]
    ext_ref[POOL_HALO:POOL_HALO + tm, :] = u
    t = i * tm + lax.broadcasted_iota(jnp.int32, (tm, 1), 0)
    parts = []
    for g, w in enumerate(POOL_WINDOWS):
        cols = slice(g * LANES, (g + 1) * LANES)
        ug = u[:, cols]
        acc = ext_ref[:, cols]
        d = 1
        while d < w:
            acc = acc + pltpu.roll(acc, d, 0)
            d *= 2
        acc = acc[POOL_HALO:, :]
        cnt = jnp.minimum(t + 1, w).astype(F32)
        pooled = acc / cnt - ug
        parts.append(_dot(pooled.astype(BF16), pool_w_ref[g]))
    y_pool = jnp.concatenate(parts, axis=-1) * pool_scale_ref[...]
    ypool_ref[0] = y_pool.astype(BF16)
    ext_ref[0:POOL_HALO, :] = ext_ref[tm:tm + POOL_HALO, :]

    ang = freq_ref[...] * pos_ref[0].astype(F32)
    cos = jnp.cos(ang)
    sin = jnp.sin(ang)
    half = QK_ROPE_DIM // 2
    r1, r2, r3 = QK_NOPE_DIM, QK_NOPE_DIM + half, QK_DIM

    def rope_t(xt):
        t1, t2 = xt[r1:r2, :], xt[r2:r3, :]
        return jnp.concatenate(
            [xt[:r1, :], t1 * cos - t2 * sin, t2 * cos + t1 * sin, xt[r3:, :]], axis=0)

    q_lat = z[:, pool_dim:pool_dim + qn_ref.shape[1]]
    qf = _dot(_rms(q_lat, qn_ref[...]).astype(BF16), wq_ref[...])
    for hd in range(MLA_HEADS):
        qt = rope_t(qf[:, hd * LANES:(hd + 1) * LANES].T)
        q_ref[0, hd] = (qt * QK_EXP2_SCALE).astype(BF16)

    kv0 = pool_dim + qn_ref.shape[1]
    kv_lat = _rms(z[:, kv0:kv0 + kvn_ref.shape[1]], kvn_ref[...]).astype(BF16)
    kf = _dot(kv_lat, wk_ref[...])
    vf = _dot(kv_lat, wv_ref[...])
    k_rope = rope_t(z[:, kv0 + kvn_ref.shape[1]:].T).T
    for hd in range(MLA_HEADS):
        k_ref[0, hd] = (kf[:, hd * LANES:(hd + 1) * LANES] + k_rope).astype(BF16)
    ones_col = ones_ref[...]
    tk = v_ref.shape[4]
    for hd in range(MLA_HEADS):
        vt = (vf[:, hd * LANES:(hd + 1) * LANES] + ones_col).T.astype(BF16)
        for u in range(tm // tk):
            v_ref[0, hd, u] = vt[:, u * tk:(u + 1) * tk]


def _even_front(x, pos_row, norm, w_in, pool_w, pool_scale, qn, wq, kvn, wk, wv, freq_col, ones_col,
                tm, tk):
    B, S, D = x.shape
    pool_dim = pool_scale.shape[1]
    const = lambda b, i: (0, 0)
    return pl.pallas_call(
        _even_front_kernel,
        grid=(B, S // tm),
        in_specs=[
            pl.BlockSpec((1, tm, D), lambda b, i: (b, i, 0)),
            pl.BlockSpec((1, 1, tm), lambda b, i: (b, 0, i)),
            pl.BlockSpec(norm.shape, const),
            pl.BlockSpec(w_in.shape, const),
            pl.BlockSpec(pool_w.shape, lambda b, i: (0, 0, 0)),
            pl.BlockSpec(pool_scale.shape, const),
            pl.BlockSpec(qn.shape, const),
            pl.BlockSpec(wq.shape, const),
            pl.BlockSpec(kvn.shape, const),
            pl.BlockSpec(wk.shape, const),
            pl.BlockSpec(wv.shape, const),
            pl.BlockSpec(freq_col.shape, const),
            pl.BlockSpec(ones_col.shape, const),
        ],
        out_specs=[
            pl.BlockSpec((1, tm, pool_dim), lambda b, i: (b, i, 0)),
            pl.BlockSpec((1, MLA_HEADS, LANES, tm), lambda b, i: (b, 0, 0, i)),
            pl.BlockSpec((1, MLA_HEADS, tm, LANES), lambda b, i: (b, 0, i, 0)),
            pl.BlockSpec((1, MLA_HEADS, tm // tk, LANES, tk), lambda b, i: (b, 0, i, 0, 0)),
        ],
        out_shape=[
            jax.ShapeDtypeStruct((B, S, pool_dim), BF16),
            jax.ShapeDtypeStruct((B, MLA_HEADS, LANES, S), BF16),
            jax.ShapeDtypeStruct((B, MLA_HEADS, S, LANES), BF16),
            jax.ShapeDtypeStruct((B, MLA_HEADS, S // tk, LANES, tk), BF16),
        ],
        scratch_shapes=[pltpu.VMEM((POOL_HALO + tm, pool_dim), F32)],
        compiler_params=_params("arbitrary", "arbitrary"),
        name="even_front",
    )(x, pos_row, norm, w_in, pool_w, pool_scale, qn, wq, kvn, wk, wv, freq_col, ones_col)


def _attn_kernel(qt_ref, k_ref, vt_ref, o_ref, m_ref, mt_ref, acc_ref, s_ref, *, tq, tk, hps):
    qi = pl.program_id(2)
    gsub = s_ref.shape[0]
    assert gsub * tk == 2 * tq
    m_ref[...] = jnp.full(m_ref.shape, NEG_INF, F32)
    acc_ref[...] = jnp.zeros(acc_ref.shape, F32)

    def stage(g_fin, n_fin, g_new, n_new, mask_off):
        for hh in range(hps):
            if n_fin:
                m_old = m_ref[hh]
                m_new = jnp.maximum(m_old, mt_ref[hh])
                alpha = jnp.exp2(m_old - m_new)
            pv = None
            mt = None
            for u in range(max(n_fin, n_new)):
                if u < n_fin:
                    p = jnp.exp2(s_ref[u, hh] - m_new).astype(BF16)
                    d = _dot(vt_ref[0, hh, g_fin * gsub + u], p)
                    pv = d if pv is None else pv + d
                if u < n_new:
                    r0 = pl.multiple_of((g_new * gsub + u) * tk, tk)
                    s = _dot(k_ref[0, hh, pl.ds(r0, tk), :], qt_ref[0, hh])
                    if mask_off is not None:
                        kk = lax.broadcasted_iota(jnp.int32, (tk, tq), 0) + (u * tk - mask_off)
                        qq = lax.broadcasted_iota(jnp.int32, (tk, tq), 1)
                        s = jnp.where(kk <= qq, s, NEG_INF)
                    s_ref[u, hh] = s
                    cm = jnp.max(s, axis=0, keepdims=True)
                    mt = cm if mt is None else jnp.maximum(mt, cm)
            if n_fin:
                acc_ref[hh] = alpha * acc_ref[hh] + pv
                m_ref[hh] = m_new
            if n_new:
                mt_ref[hh] = mt

    n_full = qi // 2
    odd = qi % 2 == 1
    even = jnp.logical_not(odd)
    some_full = n_full > 0
    none_full = n_full == 0
    half = gsub // 2

    @pl.when(some_full)
    def _():
        stage(None, 0, 0, gsub, None)

    @pl.when(jnp.logical_and(none_full, odd))
    def _():
        stage(None, 0, 0, gsub, tq)

    @pl.when(jnp.logical_and(none_full, even))
    def _():
        stage(None, 0, 0, half, 0)

    def body(g, carry):
        stage(g, gsub, g + 1, gsub, None)
        return carry

    lax.fori_loop(0, n_full - 1, body, 0)

    @pl.when(jnp.logical_and(some_full, odd))
    def _():
        stage(n_full - 1, gsub, n_full, gsub, tq)

    @pl.when(jnp.logical_and(some_full, even))
    def _():
        stage(n_full - 1, gsub, n_full, half, 0)

    @pl.when(odd)
    def _():
        stage(n_full, gsub, None, 0, None)

    @pl.when(even)
    def _():
        stage(n_full, half, None, 0, None)

    for pr in range(hps // 2):
        ot = jnp.concatenate(
            [acc_ref[hh, :V_HEAD_DIM, :] / acc_ref[hh, V_HEAD_DIM:V_HEAD_DIM + 1, :]
             for hh in (2 * pr, 2 * pr + 1)], axis=0)
        o_ref[0, :, pr * LANES:(pr + 1) * LANES] = ot.T.astype(BF16)


def _attention(qt, k, vt, tq, hps):
    B, H, S, _ = k.shape
    tk = vt.shape[-1]
    resident = pl.Buffered(1)
    return pl.pallas_call(
        functools.partial(_attn_kernel, tq=tq, tk=tk, hps=hps),
        grid=(B, H // hps, S // tq),
        in_specs=[
            pl.BlockSpec((1, hps, LANES, tq), lambda b, p, i: (b, p, 0, i)),
            pl.BlockSpec((1, hps, S, LANES), lambda b, p, i: (b, p, 0, 0), pipeline_mode=resident),
            pl.BlockSpec((1, hps, S // tk, LANES, tk), lambda b, p, i: (b, p, 0, 0, 0),
                         pipeline_mode=resident),
        ],
        out_specs=pl.BlockSpec((1, tq, (hps // 2) * LANES), lambda b, p, i: (b, i, p)),
        out_shape=jax.ShapeDtypeStruct((B, S, (H // 2) * LANES), BF16),
        scratch_shapes=[pltpu.VMEM((hps, 1, tq), F32), pltpu.VMEM((hps, 1, tq), F32),
                        pltpu.VMEM((hps, LANES, tq), F32), pltpu.VMEM((2 * tq // tk, hps, tk, tq), F32)],
        compiler_params=_params("arbitrary", "arbitrary", "arbitrary"),
        name="mla_attention",
    )(qt, k, vt)


def _gelu_tanh(x):
    return 0.5 * x * (1.0 + jnp.tanh(math.sqrt(2.0 / math.pi) * (x + 0.044715 * (x * x * x))))


def _sigmoid(x):
    return 0.5 * jnp.tanh(0.5 * x) + 0.5


def _odd_kernel(x_ref, pos_ref, norm_ref, w_in_ref, conv_w_ref, conv_b_ref, wg_ref, bg_ref,
                lam_ref, w_out_ref, out_ref, ext_ref, hcar_ref):
    i = pl.program_id(1)
    tm = x_ref.shape[1]
    width = lam_ref.shape[1]
    hd = width // LRU_HEADS
    bm = tm // ODD_ROW_BLOCKS

    @pl.when(i == 0)
    def _():
        ext_ref[0:CONV_HALO, :] = jnp.zeros((CONV_HALO, width), F32)
        hcar_ref[...] = jnp.zeros(hcar_ref.shape, F32)

    nlam = -lam_ref[...]
    softplus = jnp.maximum(nlam, 0.0) + jnp.log1p(jnp.exp(-jnp.abs(nlam)))
    sub = lax.broadcasted_iota(jnp.int32, (8, width), 0)
    keeps = {d: sub >= d for d in (1, 2, 4)}
    hprev = hcar_ref[...]
    def in_proj(blk):
        x = x_ref[0, blk * bm:(blk + 1) * bm, :]
        return x, _dot(_rms(x, norm_ref[...]).astype(BF16), w_in_ref[...])

    ahead = in_proj(0)
    for blk in range(ODD_ROW_BLOCKS):
        r0 = blk * bm
        x, z = ahead
        if blk + 1 < ODD_ROW_BLOCKS:
            ahead = in_proj(blk + 1)
        gate = z[:, :width]
        xb = z[:, width:]

        ext_ref[CONV_HALO + r0:CONV_HALO + r0 + bm, :] = xb
        xc = conv_b_ref[...] + conv_w_ref[CONV_WIDTH - 1:CONV_WIDTH, :] * xb
        for k in range(CONV_WIDTH - 1):
            off = CONV_HALO - (CONV_WIDTH - 1) + k + r0
            xc = xc + conv_w_ref[k:k + 1, :] * ext_ref[off:off + bm, :]

        xcb = xc.astype(BF16)
        r_parts, i_parts = [], []
        for h in range(LRU_HEADS):
            g = _dot(xcb[:, h * hd:(h + 1) * hd], wg_ref[h])
            r_parts.append(g[:, :hd])
            i_parts.append(g[:, hd:])
        r = _sigmoid(jnp.concatenate(r_parts, axis=-1) + bg_ref[0:1, :])
        ig = _sigmoid(jnp.concatenate(i_parts, axis=-1) + bg_ref[1:2, :])
        log_a = -LRU_C * r * softplus
        a = jnp.exp(log_a)
        mult = jnp.sqrt(jnp.maximum(-jnp.tanh(log_a) * (a * a + 1.0), 0.0))
        reset = pos_ref[0, r0:r0 + bm, :] == 0
        a = jnp.where(reset, 0.0, a)
        b = jnp.where(reset, 1.0, mult) * (ig * xc)

        hs = []
        for c in range(bm // 8):
            at = a[c * 8:(c + 1) * 8, :]
            bt = b[c * 8:(c + 1) * 8, :]
            for d in (1, 2, 4):
                a_s = jnp.where(keeps[d], pltpu.roll(at, d, 0), 1.0)
                b_s = jnp.where(keeps[d], pltpu.roll(bt, d, 0), 0.0)
                bt = at * b_s + bt
                at = at * a_s
            hcur = at * hprev + bt
            hs.append(hcur)
            hprev = jnp.broadcast_to(hcur[7:8, :], (8, width))
        y = _gelu_tanh(gate) * jnp.concatenate(hs, axis=0)
        out_ref[0, r0:r0 + bm, :] = x + _dot(y.astype(BF16), w_out_ref[...])
    ext_ref[0:CONV_HALO, :] = ext_ref[tm:tm + CONV_HALO, :]
    hcar_ref[...] = hprev


def _odd_mixer(x, pos_col, norm, w_in, conv_w, conv_b, wg, bg, lam, w_out, tm):
    B, S, D = x.shape
    width = lam.shape[1]
    const = lambda b, i: (0, 0)
    return pl.pallas_call(
        _odd_kernel,
        grid=(B, S // tm),
        in_specs=[
            pl.BlockSpec((1, tm, D), lambda b, i: (b, i, 0)),
            pl.BlockSpec((1, tm, 1), lambda b, i: (b, i, 0)),
            pl.BlockSpec(norm.shape, const),
            pl.BlockSpec(w_in.shape, const),
            pl.BlockSpec(conv_w.shape, const),
            pl.BlockSpec(conv_b.shape, const),
            pl.BlockSpec(wg.shape, lambda b, i: (0, 0, 0)),
            pl.BlockSpec(bg.shape, const),
            pl.BlockSpec(lam.shape, const),
            pl.BlockSpec(w_out.shape, const),
        ],
        out_specs=pl.BlockSpec((1, tm, D), lambda b, i: (b, i, 0)),
        out_shape=jax.ShapeDtypeStruct((B, S, D), F32),
        scratch_shapes=[pltpu.VMEM((CONV_HALO + tm, width), F32), pltpu.VMEM((8, width), F32)],
        compiler_params=_params("arbitrary", "arbitrary"),
        name="odd_mixer",
    )(x, pos_col, norm, w_in, conv_w, conv_b, wg, bg, lam, w_out)


def _memkv_kernel(mem_ref, norm_ref, w_ref, out_ref):
    out_ref[0] = _dot(_rms(mem_ref[0], norm_ref[...]).astype(BF16), w_ref[...]).astype(BF16)


def _memkv(mem, norm, w_kv):
    B, M, D = mem.shape
    return pl.pallas_call(
        _memkv_kernel,
        grid=(B,),
        in_specs=[pl.BlockSpec((1, M, D), lambda b: (b, 0, 0)),
                  pl.BlockSpec(norm.shape, lambda b: (0, 0)),
                  pl.BlockSpec(w_kv.shape, lambda b: (0, 0))],
        out_specs=pl.BlockSpec((1, M, w_kv.shape[1]), lambda b: (b, 0, 0)),
        out_shape=jax.ShapeDtypeStruct((B, M, w_kv.shape[1]), BF16),
        compiler_params=_params("arbitrary"),
        name="mem_kv",
    )(mem, norm, w_kv)


def _xattn_kernel(*refs, with_mix):
    if with_mix:
        x_ref, yp_ref, ya_ref, wop_ref, woa_ref, norm_ref, wq_ref, kv_ref, wo_ref, out_ref = refs
        x = x_ref[...] + _dot(yp_ref[...], wop_ref[...]) + _dot(ya_ref[...], woa_ref[...])
    else:
        x_ref, norm_ref, wq_ref, kv_ref, wo_ref, out_ref = refs
        x = x_ref[...]
    d = x.shape[1]
    hd = d // MEM_HEADS
    q = _dot(_rms(x, norm_ref[...]).astype(BF16), wq_ref[...]).astype(BF16)
    outs = []
    for h in range(MEM_HEADS):
        kh = kv_ref[0, :, h * hd:(h + 1) * hd]
        vh = kv_ref[0, :, d + h * hd:d + (h + 1) * hd]
        s = _dot_nt(q[:, h * hd:(h + 1) * hd], kh) * (hd ** -0.5)
        p = jnp.exp(s - jnp.max(s, axis=-1, keepdims=True))
        o = _dot(p.astype(BF16), vh) / jnp.sum(p, axis=-1, keepdims=True)
        outs.append(o.astype(BF16))
    out_ref[...] = x + _dot(jnp.concatenate(outs, axis=-1), wo_ref[...])


def _xattn(x2d, mix, norm, wq, kv, wo, tm, tiles_per_batch):
    T, D = x2d.shape
    const = lambda i: (0, 0)
    row = lambda i: (i, 0)
    args = [x2d]
    specs = [pl.BlockSpec((tm, D), row)]
    if mix is not None:
        yp, ya, wop, woa = mix
        args += [yp, ya, wop, woa]
        specs += [pl.BlockSpec((tm, yp.shape[1]), row), pl.BlockSpec((tm, ya.shape[1]), row),
                  pl.BlockSpec(wop.shape, const), pl.BlockSpec(woa.shape, const)]
    args += [norm, wq, kv, wo]
    specs += [pl.BlockSpec(norm.shape, const), pl.BlockSpec(wq.shape, const),
              pl.BlockSpec((1,) + kv.shape[1:], lambda i: (i // tiles_per_batch, 0, 0)),
              pl.BlockSpec(wo.shape, const)]
    return pl.pallas_call(
        functools.partial(_xattn_kernel, with_mix=mix is not None),
        grid=(T // tm,),
        in_specs=specs,
        out_specs=pl.BlockSpec((tm, D), row),
        out_shape=jax.ShapeDtypeStruct((T, D), F32),
        compiler_params=_params("arbitrary"),
        name="mem_xattn_mix" if mix is not None else "mem_xattn",
    )(*args)


def _ffn_kernel(x_ref, norm_ref, wgu_ref, wd_ref, fnorm_ref, out_ref, *, final):
    ff = wd_ref.shape[0]
    x = x_ref[...]
    h = _rms(x, norm_ref[...]).astype(BF16)
    g = _dot(h, wgu_ref[:, :ff])
    u = _dot(h, wgu_ref[:, ff:])
    act = (g * jax.nn.sigmoid(g) * u).astype(BF16)
    y = x + _dot(act, wd_ref[...])
    out_ref[...] = _rms(y, fnorm_ref[...]) if final else y


def _ffn(x2d, norm, w_gate_up, w_down, fnorm, tm, final):
    T, D = x2d.shape
    const = lambda i: (0, 0)
    resident = pl.Buffered(1)
    return pl.pallas_call(
        functools.partial(_ffn_kernel, final=final),
        grid=(T // tm,),
        in_specs=[
            pl.BlockSpec((tm, D), lambda i: (i, 0)),
            pl.BlockSpec(norm.shape, const),
            pl.BlockSpec(w_gate_up.shape, const, pipeline_mode=resident),
            pl.BlockSpec(w_down.shape, const, pipeline_mode=resident),
            pl.BlockSpec(fnorm.shape, const),
        ],
        out_specs=pl.BlockSpec((tm, D), lambda i: (i, 0)),
        out_shape=jax.ShapeDtypeStruct((T, D), F32),
        compiler_params=_params("arbitrary"),
        name="ffn_final" if final else "ffn",
    )(x2d, norm, w_gate_up, w_down, fnorm)


def _tiles(S):
    tm = min(512, S)
    tq = tm
    tk = min(256, tq)
    return tm, tq, tk


def _pad_cols(w, groups, width):
    k = w.shape[0]
    w = w.reshape(k, groups, -1)
    return jnp.pad(w, ((0, 0), (0, 0), (0, width - w.shape[2]))).reshape(k, groups * width)


def kernel(x, mem, positions, ev_norm, ev_w_in, ev_pool_w, ev_pool_scale, ev_q_norm, ev_w_q_up, ev_kv_norm, ev_w_kv_up, ev_w_out, od_norm, od_w_in, od_conv_w, od_conv_b, od_w_rgate, od_b_rgate, od_w_igate, od_b_igate, od_lambda, od_w_out, xa_norm_x, xa_norm_mem, xa_w_q, xa_w_kv, xa_w_o, ffn_norm, ffn_w_gate_up, ffn_w_down, final_norm):
    B, S, D = x.shape
    depth = xa_w_q.shape[0]
    ff = ffn_w_down.shape[1]
    tm, tq, tk = _tiles(S)
    pos_col = positions.reshape(B, S, 1)
    pool_dim = ev_pool_scale.shape[1]
    q_rank = ev_q_norm.shape[1]
    kv_rank = ev_kv_norm.shape[1]

    pos_row = positions.reshape(B, 1, S)
    inv_freq = ROPE_BASE ** (-jnp.arange(0, QK_ROPE_DIM, 2, dtype=F32) / QK_ROPE_DIM)
    freq_col = inv_freq.reshape(-1, 1)
    ones_col = jnp.zeros((1, LANES), F32).at[0, V_HEAD_DIM].set(1.0)

    row = lambda v: v.reshape(1, -1)
    h = x
    for layer in range(depth):
        j = layer // 2
        if layer % 2 == 0:
            lat = pool_dim + q_rank + kv_rank
            w_in = jnp.concatenate(
                [ev_w_in[j][:, :lat], jnp.zeros((D, QK_NOPE_DIM), F32), ev_w_in[j][:, lat:],
                 jnp.zeros((D, LANES - QK_DIM), F32)], axis=1).astype(BF16)
            wq = _pad_cols(ev_w_q_up[j], MLA_HEADS, LANES).astype(BF16)
            wkv = ev_w_kv_up[j].reshape(kv_rank, MLA_HEADS, QK_NOPE_DIM + V_HEAD_DIM)
            wk = _pad_cols(wkv[:, :, :QK_NOPE_DIM].reshape(kv_rank, -1), MLA_HEADS, LANES).astype(BF16)
            wv = _pad_cols(wkv[:, :, QK_NOPE_DIM:].reshape(kv_rank, -1), MLA_HEADS, LANES).astype(BF16)
            y_pool, q, k, v = _even_front(
                h, pos_row, row(ev_norm[j]), w_in, ev_pool_w[j].astype(BF16), row(ev_pool_scale[j]),
                row(ev_q_norm[j]), wq, row(ev_kv_norm[j]), wk, wv, freq_col, ones_col, tm, tk)
            y_att = _attention(q, k, v, tq, ATTN_HEADS_PER_STEP)
            w_out = ev_w_out[j].astype(BF16)
            mix = (y_pool.reshape(B * S, -1), y_att.reshape(B * S, -1),
                   w_out[:pool_dim], w_out[pool_dim:])
            h2d = h.reshape(B * S, D)
        else:
            lw = od_lambda.shape[1]
            hd = lw // LRU_HEADS
            wg = jnp.concatenate([od_w_rgate[j], od_w_igate[j]], axis=-1).astype(BF16)
            bg = jnp.stack([od_b_rgate[j], od_b_igate[j]])
            h = _odd_mixer(h, pos_col, row(od_norm[j]), od_w_in[j].astype(BF16), od_conv_w[j],
                           row(od_conv_b[j]), wg, bg, row(od_lambda[j]), od_w_out[j].astype(BF16), tm)
            mix = None
            h2d = h.reshape(B * S, D)
        kv = _memkv(mem, row(xa_norm_mem[layer]), xa_w_kv[layer].astype(BF16))
        h2d = _xattn(h2d, mix, row(xa_norm_x[layer]), xa_w_q[layer].astype(BF16), kv,
                     xa_w_o[layer].astype(BF16), tm, S // tm)
        h2d = _ffn(h2d, row(ffn_norm[layer]), ffn_w_gate_up[layer].astype(BF16),
                   ffn_w_down[layer].astype(BF16), row(final_norm), tm,
                   final=layer == depth - 1)
        h = h2d.reshape(B, S, D)
    return h
```

```python
import functools
import math

import jax
import jax.numpy as jnp
from jax import lax
from jax.experimental import pallas as pl
from jax.experimental.pallas import tpu as pltpu

F32 = jnp.float32
BF16 = jnp.bfloat16

LANES = 128

POOL_WINDOWS = (2, 4, 8, 16)
POOL_HALO = 16
MLA_HEADS = 8
QK_NOPE_DIM = 64
QK_ROPE_DIM = 32
QK_DIM = QK_NOPE_DIM + QK_ROPE_DIM
V_HEAD_DIM = 64
ATTN_HEADS_PER_STEP = 4
QK_EXP2_SCALE = (QK_DIM ** -0.5) * math.log2(math.e)
ROPE_BASE = 10000.0
LRU_HEADS = 4
CONV_WIDTH = 4
CONV_HALO = 8
ODD_ROW_BLOCKS = 2
LRU_C = 8.0
MEM_HEADS = 4
RMS_EPS = 1e-6
NEG_INF = -1e30

VMEM_LIMIT_BYTES = 56 * 1024 * 1024


def _params(*sem):
    return pltpu.CompilerParams(dimension_semantics=sem, vmem_limit_bytes=VMEM_LIMIT_BYTES)


def _rms(x, g):
    return x * lax.rsqrt(jnp.mean(x * x, axis=-1, keepdims=True) + RMS_EPS) * g


def _dot(a, b):
    return jnp.dot(a, b, preferred_element_type=F32)


def _dot_nt(a, b):
    return lax.dot_general(a, b, (((1,), (1,)), ((), ())), preferred_element_type=F32)


def _even_front_kernel(x_ref, pos_ref, norm_ref, w_in_ref, pool_w_ref, pool_scale_ref,
                       qn_ref, wq_ref, kvn_ref, wk_ref, wv_ref, freq_ref, ones_ref,
                       ypool_ref, q_ref, k_ref, v_ref, ext_ref):
    i = pl.program_id(1)
    tm = x_ref.shape[1]
    pool_dim = ypool_ref.shape[2]
    h = _rms(x_ref[0], norm_ref[...]).astype(BF16)
    z = _dot(h, w_in_ref[...])

    @pl.when(i == 0)
    def _():
        ext_ref[0:POOL_HALO, :] = jnp.zeros((POOL_HALO, pool_dim), F32)

    u = z[:, :pool_dim]
    ext_ref[POOL_HALO:POOL_HALO + tm, :] = u
    t = i * tm + lax.broadcasted_iota(jnp.int32, (tm, 1), 0)
    parts = []
    for g, w in enumerate(POOL_WINDOWS):
        cols = slice(g * LANES, (g + 1) * LANES)
        ug = u[:, cols]
        acc = ext_ref[:, cols]
        d = 1
        while d < w:
            acc = acc + pltpu.roll(acc, d, 0)
            d *= 2
        acc = acc[POOL_HALO:, :]
        cnt = jnp.minimum(t + 1, w).astype(F32)
        pooled = acc / cnt - ug
        parts.append(_dot(pooled.astype(BF16), pool_w_ref[g]))
    y_pool = jnp.concatenate(parts, axis=-1) * pool_scale_ref[...]
    ypool_ref[0] = y_pool.astype(BF16)
    ext_ref[0:POOL_HALO, :] = ext_ref[tm:tm + POOL_HALO, :]

    ang = freq_ref[...] * pos_ref[0].astype(F32)
    cos = jnp.cos(ang)
    sin = jnp.sin(ang)
    half = QK_ROPE_DIM // 2
    r1, r2, r3 = QK_NOPE_DIM, QK_NOPE_DIM + half, QK_DIM

    def rope_t(xt):
        t1, t2 = xt[r1:r2, :], xt[r2:r3, :]
        return jnp.concatenate(
            [xt[:r1, :], t1 * cos - t2 * sin, t2 * cos + t1 * sin, xt[r3:, :]], axis=0)

    q_lat = z[:, pool_dim:pool_dim + qn_ref.shape[1]]
    qf = _dot(_rms(q_lat, qn_ref[...]).astype(BF16), wq_ref[...])
    for hd in range(MLA_HEADS):
        qt = rope_t(qf[:, hd * LANES:(hd + 1) * LANES].T)
        q_ref[0, hd] = (qt * QK_EXP2_SCALE).astype(BF16)

    kv0 = pool_dim + qn_ref.shape[1]
    kv_lat = _rms(z[:, kv0:kv0 + kvn_ref.shape[1]], kvn_ref[...]).astype(BF16)
    kf = _dot(kv_lat, wk_ref[...])
    vf = _dot(kv_lat, wv_ref[...])
    k_rope = rope_t(z[:, kv0 + kvn_ref.shape[1]:].T).T
    for hd in range(MLA_HEADS):
        k_ref[0, hd] = (kf[:, hd * LANES:(hd + 1) * LANES] + k_rope).astype(BF16)
    ones_col = ones_ref[...]
    tk = v_ref.shape[4]
    for hd in range(MLA_HEADS):
        vt = (vf[:, hd * LANES:(hd + 1) * LANES] + ones_col).T.astype(BF16)
        for u in range(tm // tk):
            v_ref[0, hd, u] = vt[:, u * tk:(u + 1) * tk]


def _even_front(x, pos_row, norm, w_in, pool_w, pool_scale, qn, wq, kvn, wk, wv, freq_col, ones_col,
                tm, tk):
    B, S, D = x.shape
    pool_dim = pool_scale.shape[1]
    const = lambda b, i: (0, 0)
    return pl.pallas_call(
        _even_front_kernel,
        grid=(B, S // tm),
        in_specs=[
            pl.BlockSpec((1, tm, D), lambda b, i: (b, i, 0)),
            pl.BlockSpec((1, 1, tm), lambda b, i: (b, 0, i)),
            pl.BlockSpec(norm.shape, const),
            pl.BlockSpec(w_in.shape, const),
            pl.BlockSpec(pool_w.shape, lambda b, i: (0, 0, 0)),
            pl.BlockSpec(pool_scale.shape, const),
            pl.BlockSpec(qn.shape, const),
            pl.BlockSpec(wq.shape, const),
            pl.BlockSpec(kvn.shape, const),
            pl.BlockSpec(wk.shape, const),
            pl.BlockSpec(wv.shape, const),
            pl.BlockSpec(freq_col.shape, const),
            pl.BlockSpec(ones_col.shape, const),
        ],
        out_specs=[
            pl.BlockSpec((1, tm, pool_dim), lambda b, i: (b, i, 0)),
            pl.BlockSpec((1, MLA_HEADS, LANES, tm), lambda b, i: (b, 0, 0, i)),
            pl.BlockSpec((1, MLA_HEADS, tm, LANES), lambda b, i: (b, 0, i, 0)),
            pl.BlockSpec((1, MLA_HEADS, tm // tk, LANES, tk), lambda b, i: (b, 0, i, 0, 0)),
        ],
        out_shape=[
            jax.ShapeDtypeStruct((B, S, pool_dim), BF16),
            jax.ShapeDtypeStruct((B, MLA_HEADS, LANES, S), BF16),
            jax.ShapeDtypeStruct((B, MLA_HEADS, S, LANES), BF16),
            jax.ShapeDtypeStruct((B, MLA_HEADS, S // tk, LANES, tk), BF16),
        ],
        scratch_shapes=[pltpu.VMEM((POOL_HALO + tm, pool_dim), F32)],
        compiler_params=_params("arbitrary", "arbitrary"),
        name="even_front",
    )(x, pos_row, norm, w_in, pool_w, pool_scale, qn, wq, kvn, wk, wv, freq_col, ones_col)


def _attn_kernel(qt_ref, qn_ref, k_ref, vt_ref, o_ref, m_ref, mt_ref, acc_ref, s_ref, *, tq, tk, hps):
    qi = pl.program_id(2)
    gsub = s_ref.shape[0]
    assert gsub * tk == 2 * tq
    m_ref[...] = jnp.full(m_ref.shape, NEG_INF, F32)
    acc_ref[...] = jnp.zeros(acc_ref.shape, F32)

    def stage(g_fin, n_fin, g_new, n_new, mask_off, q_ref=qt_ref):
        for hh in range(hps):
            if n_fin:
                m_old = m_ref[hh]
                m_new = jnp.maximum(m_old, mt_ref[hh])
                alpha = jnp.exp2(m_old - m_new)
            pv = None
            mt = None
            for u in range(max(n_fin, n_new)):
                if u < n_fin:
                    p = jnp.exp2(s_ref[u, hh] - m_new).astype(BF16)
                    d = _dot(vt_ref[0, hh, g_fin * gsub + u], p)
                    pv = d if pv is None else pv + d
                if u < n_new:
                    r0 = pl.multiple_of((g_new * gsub + u) * tk, tk)
                    s = _dot(k_ref[0, hh, pl.ds(r0, tk), :], q_ref[0, hh])
                    if mask_off is not None:
                        kk = lax.broadcasted_iota(jnp.int32, (tk, tq), 0) + (u * tk - mask_off)
                        qq = lax.broadcasted_iota(jnp.int32, (tk, tq), 1)
                        s = jnp.where(kk <= qq, s, NEG_INF)
                    s_ref[u, hh] = s
                    cm = jnp.max(s, axis=0, keepdims=True)
                    mt = cm if mt is None else jnp.maximum(mt, cm)
            if n_fin:
                acc_ref[hh] = alpha * acc_ref[hh] + pv
                m_ref[hh] = m_new
            if n_new:
                mt_ref[hh] = mt

    n_full = qi // 2
    odd = qi % 2 == 1
    even = jnp.logical_not(odd)
    some_full = n_full > 0
    first = qi == 0
    last = qi == pl.num_programs(2) - 1
    inner = jnp.logical_not(jnp.logical_or(first, last))
    half = gsub // 2

    @pl.when(first)
    def _():
        stage(None, 0, 0, half, 0)

    def body(t, carry):
        stage(2 * t, gsub, 2 * t + 1, gsub, None)
        stage(2 * t + 1, gsub, 2 * t + 2, gsub, None)
        return carry

    steady = jnp.maximum(n_full - 1, 0)
    lax.fori_loop(0, steady // 2, body, 0)

    @pl.when(steady % 2 == 1)
    def _():
        stage(steady - 1, gsub, steady, gsub, None)

    @pl.when(jnp.logical_and(some_full, odd))
    def _():
        stage(n_full - 1, gsub, n_full, gsub, tq)

    @pl.when(jnp.logical_and(some_full, even))
    def _():
        stage(n_full - 1, gsub, n_full, half, 0)

    for parity, n_fin in ((odd, gsub), (even, half)):
        @pl.when(jnp.logical_and(parity, last))
        def _():
            stage(n_full, n_fin, None, 0, None)

        @pl.when(jnp.logical_and(parity, inner))
        def _():
            stage(n_full, n_fin, 0, gsub, None, qn_ref)

    @pl.when(jnp.logical_and(first, jnp.logical_not(last)))
    def _():
        stage(n_full, half, 0, gsub, tq, qn_ref)

    for pr in range(hps // 2):
        ot = jnp.concatenate(
            [acc_ref[hh, :V_HEAD_DIM, :] / acc_ref[hh, V_HEAD_DIM:V_HEAD_DIM + 1, :]
             for hh in (2 * pr, 2 * pr + 1)], axis=0)
        o_ref[0, :, pr * LANES:(pr + 1) * LANES] = ot.T.astype(BF16)


def _attention(qt, k, vt, tq, hps):
    B, H, S, _ = k.shape
    tk = vt.shape[-1]
    resident = pl.Buffered(1)
    nq = S // tq
    return pl.pallas_call(
        functools.partial(_attn_kernel, tq=tq, tk=tk, hps=hps),
        grid=(B, H // hps, nq),
        in_specs=[
            pl.BlockSpec((1, hps, LANES, tq), lambda b, p, i: (b, p, 0, i)),
            pl.BlockSpec((1, hps, LANES, tq), lambda b, p, i: (b, p, 0, jnp.minimum(i + 1, nq - 1))),
            pl.BlockSpec((1, hps, S, LANES), lambda b, p, i: (b, p, 0, 0), pipeline_mode=resident),
            pl.BlockSpec((1, hps, S // tk, LANES, tk), lambda b, p, i: (b, p, 0, 0, 0),
                         pipeline_mode=resident),
        ],
        out_specs=pl.BlockSpec((1, tq, (hps // 2) * LANES), lambda b, p, i: (b, i, p)),
        out_shape=jax.ShapeDtypeStruct((B, S, (H // 2) * LANES), BF16),
        scratch_shapes=[pltpu.VMEM((hps, 1, tq), F32), pltpu.VMEM((hps, 1, tq), F32),
                        pltpu.VMEM((hps, LANES, tq), F32), pltpu.VMEM((2 * tq // tk, hps, tk, tq), F32)],
        compiler_params=_params("arbitrary", "arbitrary", "arbitrary"),
        name="mla_attention",
    )(qt, qt, k, vt)


def _gelu_tanh(x):
    return 0.5 * x * (1.0 + jnp.tanh(math.sqrt(2.0 / math.pi) * (x + 0.044715 * (x * x * x))))


def _sigmoid(x):
    return 0.5 * jnp.tanh(0.5 * x) + 0.5


def _odd_kernel(x_ref, pos_ref, norm_ref, w_in_ref, conv_w_ref, conv_b_ref, wg_ref, bg_ref,
                lam_ref, w_out_ref, out_ref, ext_ref, hcar_ref):
    i = pl.program_id(1)
    tm = x_ref.shape[1]
    width = lam_ref.shape[1]
    hd = width // LRU_HEADS
    bm = tm // ODD_ROW_BLOCKS

    @pl.when(i == 0)
    def _():
        ext_ref[0:CONV_HALO, :] = jnp.zeros((CONV_HALO, width), F32)
        hcar_ref[...] = jnp.zeros(hcar_ref.shape, F32)

    nlam = -lam_ref[...]
    softplus = jnp.maximum(nlam, 0.0) + jnp.log1p(jnp.exp(-jnp.abs(nlam)))
    sub = lax.broadcasted_iota(jnp.int32, (8, width), 0)
    keeps = {d: sub >= d for d in (1, 2, 4)}
    hprev = hcar_ref[...]
    def in_proj(blk):
        x = x_ref[0, blk * bm:(blk + 1) * bm, :]
        return x, _dot(_rms(x, norm_ref[...]).astype(BF16), w_in_ref[...])

    ahead = in_proj(0)
    for blk in range(ODD_ROW_BLOCKS):
        r0 = blk * bm
        x, z = ahead
        if blk + 1 < ODD_ROW_BLOCKS:
            ahead = in_proj(blk + 1)
        gate = z[:, :width]
        xb = z[:, width:]

        ext_ref[CONV_HALO + r0:CONV_HALO + r0 + bm, :] = xb
        xc = conv_b_ref[...] + conv_w_ref[CONV_WIDTH - 1:CONV_WIDTH, :] * xb
        for k in range(CONV_WIDTH - 1):
            off = CONV_HALO - (CONV_WIDTH - 1) + k + r0
            xc = xc + conv_w_ref[k:k + 1, :] * ext_ref[off:off + bm, :]

        xcb = xc.astype(BF16)
        r_parts, i_parts = [], []
        for h in range(LRU_HEADS):
            g = _dot(xcb[:, h * hd:(h + 1) * hd], wg_ref[h])
            r_parts.append(g[:, :hd])
            i_parts.append(g[:, hd:])
        r = _sigmoid(jnp.concatenate(r_parts, axis=-1) + bg_ref[0:1, :])
        ig = _sigmoid(jnp.concatenate(i_parts, axis=-1) + bg_ref[1:2, :])
        log_a = -LRU_C * r * softplus
        a = jnp.exp(log_a)
        mult = jnp.sqrt(jnp.maximum(-jnp.tanh(log_a) * (a * a + 1.0), 0.0))
        reset = pos_ref[0, r0:r0 + bm, :] == 0
        a = jnp.where(reset, 0.0, a)
        b = jnp.where(reset, 1.0, mult) * (ig * xc)

        hs = []
        for c in range(bm // 8):
            at = a[c * 8:(c + 1) * 8, :]
            bt = b[c * 8:(c + 1) * 8, :]
            for d in (1, 2, 4):
                a_s = jnp.where(keeps[d], pltpu.roll(at, d, 0), 1.0)
                b_s = jnp.where(keeps[d], pltpu.roll(bt, d, 0), 0.0)
                bt = at * b_s + bt
                at = at * a_s
            hcur = at * hprev + bt
            hs.append(hcur)
            hprev = jnp.broadcast_to(hcur[7:8, :], (8, width))
        y = _gelu_tanh(gate) * jnp.concatenate(hs, axis=0)
        out_ref[0, r0:r0 + bm, :] = x + _dot(y.astype(BF16), w_out_ref[...])
    ext_ref[0:CONV_HALO, :] = ext_ref[tm:tm + CONV_HALO, :]
    hcar_ref[...] = hprev


def _odd_mixer(x, pos_col, norm, w_in, conv_w, conv_b, wg, bg, lam, w_out, tm):
    B, S, D = x.shape
    width = lam.shape[1]
    const = lambda b, i: (0, 0)
    return pl.pallas_call(
        _odd_kernel,
        grid=(B, S // tm),
        in_specs=[
            pl.BlockSpec((1, tm, D), lambda b, i: (b, i, 0)),
            pl.BlockSpec((1, tm, 1), lambda b, i: (b, i, 0)),
            pl.BlockSpec(norm.shape, const),
            pl.BlockSpec(w_in.shape, const),
            pl.BlockSpec(conv_w.shape, const),
            pl.BlockSpec(conv_b.shape, const),
            pl.BlockSpec(wg.shape, lambda b, i: (0, 0, 0)),
            pl.BlockSpec(bg.shape, const),
            pl.BlockSpec(lam.shape, const),
            pl.BlockSpec(w_out.shape, const),
        ],
        out_specs=pl.BlockSpec((1, tm, D), lambda b, i: (b, i, 0)),
        out_shape=jax.ShapeDtypeStruct((B, S, D), F32),
        scratch_shapes=[pltpu.VMEM((CONV_HALO + tm, width), F32), pltpu.VMEM((8, width), F32)],
        compiler_params=_params("arbitrary", "arbitrary"),
        name="odd_mixer",
    )(x, pos_col, norm, w_in, conv_w, conv_b, wg, bg, lam, w_out)


def _memkv_kernel(mem_ref, norm_ref, w_ref, out_ref):
    out_ref[0] = _dot(_rms(mem_ref[0], norm_ref[...]).astype(BF16), w_ref[...]).astype(BF16)


def _memkv(mem, norm, w_kv):
    B, M, D = mem.shape
    return pl.pallas_call(
        _memkv_kernel,
        grid=(B,),
        in_specs=[pl.BlockSpec((1, M, D), lambda b: (b, 0, 0)),
                  pl.BlockSpec(norm.shape, lambda b: (0, 0)),
                  pl.BlockSpec(w_kv.shape, lambda b: (0, 0))],
        out_specs=pl.BlockSpec((1, M, w_kv.shape[1]), lambda b: (b, 0, 0)),
        out_shape=jax.ShapeDtypeStruct((B, M, w_kv.shape[1]), BF16),
        compiler_params=_params("arbitrary"),
        name="mem_kv",
    )(mem, norm, w_kv)


def _xattn_kernel(*refs, with_mix):
    if with_mix:
        x_ref, yp_ref, ya_ref, wop_ref, woa_ref, norm_ref, wq_ref, kv_ref, wo_ref, out_ref = refs
        x = x_ref[...] + _dot(yp_ref[...], wop_ref[...]) + _dot(ya_ref[...], woa_ref[...])
    else:
        x_ref, norm_ref, wq_ref, kv_ref, wo_ref, out_ref = refs
        x = x_ref[...]
    d = x.shape[1]
    hd = d // MEM_HEADS
    q = _dot(_rms(x, norm_ref[...]).astype(BF16), wq_ref[...]).astype(BF16)
    outs = []
    for h in range(MEM_HEADS):
        kh = kv_ref[0, :, h * hd:(h + 1) * hd]
        vh = kv_ref[0, :, d + h * hd:d + (h + 1) * hd]
        s = _dot_nt(q[:, h * hd:(h + 1) * hd], kh) * (hd ** -0.5)
        p = jnp.exp(s - jnp.max(s, axis=-1, keepdims=True))
        o = _dot(p.astype(BF16), vh) / jnp.sum(p, axis=-1, keepdims=True)
        outs.append(o.astype(BF16))
    out_ref[...] = x + _dot(jnp.concatenate(outs, axis=-1), wo_ref[...])


def _xattn(x2d, mix, norm, wq, kv, wo, tm, tiles_per_batch):
    T, D = x2d.shape
    const = lambda i: (0, 0)
    row = lambda i: (i, 0)
    args = [x2d]
    specs = [pl.BlockSpec((tm, D), row)]
    if mix is not None:
        yp, ya, wop, woa = mix
        args += [yp, ya, wop, woa]
        specs += [pl.BlockSpec((tm, yp.shape[1]), row), pl.BlockSpec((tm, ya.shape[1]), row),
                  pl.BlockSpec(wop.shape, const), pl.BlockSpec(woa.shape, const)]
    args += [norm, wq, kv, wo]
    specs += [pl.BlockSpec(norm.shape, const), pl.BlockSpec(wq.shape, const),
              pl.BlockSpec((1,) + kv.shape[1:], lambda i: (i // tiles_per_batch, 0, 0)),
              pl.BlockSpec(wo.shape, const)]
    return pl.pallas_call(
        functools.partial(_xattn_kernel, with_mix=mix is not None),
        grid=(T // tm,),
        in_specs=specs,
        out_specs=pl.BlockSpec((tm, D), row),
        out_shape=jax.ShapeDtypeStruct((T, D), F32),
        compiler_params=_params("arbitrary"),
        name="mem_xattn_mix" if mix is not None else "mem_xattn",
    )(*args)


def _ffn_kernel(x_ref, norm_ref, wgu_ref, wd_ref, fnorm_ref, out_ref, *, final):
    ff = wd_ref.shape[0]
    x = x_ref[...]
    h = _rms(x, norm_ref[...]).astype(BF16)
    g = _dot(h, wgu_ref[:, :ff])
    u = _dot(h, wgu_ref[:, ff:])
    act = (g * jax.nn.sigmoid(g) * u).astype(BF16)
    y = x + _dot(act, wd_ref[...])
    out_ref[...] = _rms(y, fnorm_ref[...]) if final else y


def _ffn(x2d, norm, w_gate_up, w_down, fnorm, tm, final):
    T, D = x2d.shape
    const = lambda i: (0, 0)
    resident = pl.Buffered(1)
    return pl.pallas_call(
        functools.partial(_ffn_kernel, final=final),
        grid=(T // tm,),
        in_specs=[
            pl.BlockSpec((tm, D), lambda i: (i, 0)),
            pl.BlockSpec(norm.shape, const),
            pl.BlockSpec(w_gate_up.shape, const, pipeline_mode=resident),
            pl.BlockSpec(w_down.shape, const, pipeline_mode=resident),
            pl.BlockSpec(fnorm.shape, const),
        ],
        out_specs=pl.BlockSpec((tm, D), lambda i: (i, 0)),
        out_shape=jax.ShapeDtypeStruct((T, D), F32),
        compiler_params=_params("arbitrary"),
        name="ffn_final" if final else "ffn",
    )(x2d, norm, w_gate_up, w_down, fnorm)


def _tiles(S):
    tm = min(512, S)
    tq = tm
    tk = min(256, tq)
    return tm, tq, tk


def _pad_cols(w, groups, width):
    k = w.shape[0]
    w = w.reshape(k, groups, -1)
    return jnp.pad(w, ((0, 0), (0, 0), (0, width - w.shape[2]))).reshape(k, groups * width)


def kernel(x, mem, positions, ev_norm, ev_w_in, ev_pool_w, ev_pool_scale, ev_q_norm, ev_w_q_up, ev_kv_norm, ev_w_kv_up, ev_w_out, od_norm, od_w_in, od_conv_w, od_conv_b, od_w_rgate, od_b_rgate, od_w_igate, od_b_igate, od_lambda, od_w_out, xa_norm_x, xa_norm_mem, xa_w_q, xa_w_kv, xa_w_o, ffn_norm, ffn_w_gate_up, ffn_w_down, final_norm):
    B, S, D = x.shape
    depth = xa_w_q.shape[0]
    ff = ffn_w_down.shape[1]
    tm, tq, tk = _tiles(S)
    pos_col = positions.reshape(B, S, 1)
    pool_dim = ev_pool_scale.shape[1]
    q_rank = ev_q_norm.shape[1]
    kv_rank = ev_kv_norm.shape[1]

    pos_row = positions.reshape(B, 1, S)
    inv_freq = ROPE_BASE ** (-jnp.arange(0, QK_ROPE_DIM, 2, dtype=F32) / QK_ROPE_DIM)
    freq_col = inv_freq.reshape(-1, 1)
    ones_col = jnp.zeros((1, LANES), F32).at[0, V_HEAD_DIM].set(1.0)

    row = lambda v: v.reshape(1, -1)
    h = x
    for layer in range(depth):
        j = layer // 2
        if layer % 2 == 0:
            lat = pool_dim + q_rank + kv_rank
            w_in = jnp.concatenate(
                [ev_w_in[j][:, :lat], jnp.zeros((D, QK_NOPE_DIM), F32), ev_w_in[j][:, lat:],
                 jnp.zeros((D, LANES - QK_DIM), F32)], axis=1).astype(BF16)
            wq = _pad_cols(ev_w_q_up[j], MLA_HEADS, LANES).astype(BF16)
            wkv = ev_w_kv_up[j].reshape(kv_rank, MLA_HEADS, QK_NOPE_DIM + V_HEAD_DIM)
            wk = _pad_cols(wkv[:, :, :QK_NOPE_DIM].reshape(kv_rank, -1), MLA_HEADS, LANES).astype(BF16)
            wv = _pad_cols(wkv[:, :, QK_NOPE_DIM:].reshape(kv_rank, -1), MLA_HEADS, LANES).astype(BF16)
            y_pool, q, k, v = _even_front(
                h, pos_row, row(ev_norm[j]), w_in, ev_pool_w[j].astype(BF16), row(ev_pool_scale[j]),
                row(ev_q_norm[j]), wq, row(ev_kv_norm[j]), wk, wv, freq_col, ones_col, tm, tk)
            y_att = _attention(q, k, v, tq, ATTN_HEADS_PER_STEP)
            w_out = ev_w_out[j].astype(BF16)
            mix = (y_pool.reshape(B * S, -1), y_att.reshape(B * S, -1),
                   w_out[:pool_dim], w_out[pool_dim:])
            h2d = h.reshape(B * S, D)
        else:
            lw = od_lambda.shape[1]
            hd = lw // LRU_HEADS
            wg = jnp.concatenate([od_w_rgate[j], od_w_igate[j]], axis=-1).astype(BF16)
            bg = jnp.stack([od_b_rgate[j], od_b_igate[j]])
            h = _odd_mixer(h, pos_col, row(od_norm[j]), od_w_in[j].astype(BF16), od_conv_w[j],
                           row(od_conv_b[j]), wg, bg, row(od_lambda[j]), od_w_out[j].astype(BF16), tm)
            mix = None
            h2d = h.reshape(B * S, D)
        kv = _memkv(mem, row(xa_norm_mem[layer]), xa_w_kv[layer].astype(BF16))
        h2d = _xattn(h2d, mix, row(xa_norm_x[layer]), xa_w_q[layer].astype(BF16), kv,
                     xa_w_o[layer].astype(BF16), tm, S // tm)
        h2d = _ffn(h2d, row(ffn_norm[layer]), ffn_w_gate_up[layer].astype(BF16),
                   ffn_w_down[layer].astype(BF16), row(final_norm), tm,
                   final=layer == depth - 1)
        h = h2d.reshape(B, S, D)
    return h
```

```python
import functools
import math

import jax
import jax.numpy as jnp
from jax import lax
from jax.experimental import pallas as pl
from jax.experimental.pallas import tpu as pltpu

F32 = jnp.float32
BF16 = jnp.bfloat16

LANES = 128

POOL_WINDOWS = (2, 4, 8, 16)
POOL_HALO = 16
MLA_HEADS = 8
QK_NOPE_DIM = 64
QK_ROPE_DIM = 32
QK_DIM = QK_NOPE_DIM + QK_ROPE_DIM
V_HEAD_DIM = 64
ATTN_HEADS_PER_STEP = 4
QK_EXP2_SCALE = (QK_DIM ** -0.5) * math.log2(math.e)
ROPE_BASE = 10000.0
LRU_HEADS = 4
CONV_WIDTH = 4
CONV_HALO = 8
ODD_BLOCK_ROWS = 256
LRU_C = 8.0
MEM_HEADS = 4
RMS_EPS = 1e-6
NEG_INF = -1e30

VMEM_LIMIT_BYTES = 56 * 1024 * 1024


def _params(*sem):
    return pltpu.CompilerParams(dimension_semantics=sem, vmem_limit_bytes=VMEM_LIMIT_BYTES)


def _rms(x, g):
    return x * lax.rsqrt(jnp.mean(x * x, axis=-1, keepdims=True) + RMS_EPS) * g


def _dot(a, b):
    return jnp.dot(a, b, preferred_element_type=F32)


def _dot_nt(a, b):
    return lax.dot_general(a, b, (((1,), (1,)), ((), ())), preferred_element_type=F32)


def _even_front_kernel(x_ref, pos_ref, norm_ref, w_in_ref, pool_w_ref, pool_scale_ref,
                       qn_ref, wq_ref, kvn_ref, wk_ref, wv_ref, freq_ref, ones_ref,
                       ypool_ref, q_ref, k_ref, v_ref, ext_ref):
    i = pl.program_id(1)
    tm = x_ref.shape[1]
    pool_dim = ypool_ref.shape[2]
    bm = v_ref.shape[4]
    half = QK_ROPE_DIM // 2
    r1, r2, r3 = QK_NOPE_DIM, QK_NOPE_DIM + half, QK_DIM
    kv0 = pool_dim + qn_ref.shape[1]
    ones_col = ones_ref[...]

    @pl.when(i == 0)
    def _():
        ext_ref[0:POOL_HALO, :] = jnp.zeros((POOL_HALO, pool_dim), F32)

    def in_proj(blk):
        h = _rms(x_ref[0, blk * bm:(blk + 1) * bm, :], norm_ref[...]).astype(BF16)
        return _dot(h, w_in_ref[...])

    ahead = in_proj(0)
    for blk in range(tm // bm):
        r0 = blk * bm
        z = ahead
        if (blk + 1) * bm < tm:
            ahead = in_proj(blk + 1)

        u = z[:, :pool_dim]
        ext_ref[POOL_HALO + r0:POOL_HALO + r0 + bm, :] = u
        t = i * tm + r0 + lax.broadcasted_iota(jnp.int32, (bm, 1), 0)
        parts = []
        for g, w in enumerate(POOL_WINDOWS):
            cols = slice(g * LANES, (g + 1) * LANES)
            acc = ext_ref[r0:r0 + POOL_HALO + bm, cols]
            d = 1
            while d < w:
                acc = acc + pltpu.roll(acc, d, 0)
                d *= 2
            cnt = jnp.minimum(t + 1, w).astype(F32)
            pooled = acc[POOL_HALO:, :] / cnt - u[:, cols]
            parts.append(_dot(pooled.astype(BF16), pool_w_ref[g]))
        y_pool = jnp.concatenate(parts, axis=-1) * pool_scale_ref[...]
        ypool_ref[0, r0:r0 + bm, :] = y_pool.astype(BF16)

        ang = freq_ref[...] * pos_ref[0, :, r0:r0 + bm].astype(F32)
        cos = jnp.cos(ang)
        sin = jnp.sin(ang)

        def rope_t(xt):
            t1, t2 = xt[r1:r2, :], xt[r2:r3, :]
            return jnp.concatenate(
                [xt[:r1, :], t1 * cos - t2 * sin, t2 * cos + t1 * sin, xt[r3:, :]], axis=0)

        qf = _dot(_rms(z[:, pool_dim:kv0], qn_ref[...]).astype(BF16), wq_ref[...])
        for hd in range(MLA_HEADS):
            qt = rope_t(qf[:, hd * LANES:(hd + 1) * LANES].T)
            q_ref[0, hd, :, r0:r0 + bm] = (qt * QK_EXP2_SCALE).astype(BF16)

        kv_lat = _rms(z[:, kv0:kv0 + kvn_ref.shape[1]], kvn_ref[...]).astype(BF16)
        kf = _dot(kv_lat, wk_ref[...])
        vf = _dot(kv_lat, wv_ref[...])
        k_rope = rope_t(z[:, kv0 + kvn_ref.shape[1]:].T).T
        for hd in range(MLA_HEADS):
            k_ref[0, hd, r0:r0 + bm, :] = (kf[:, hd * LANES:(hd + 1) * LANES] + k_rope).astype(BF16)
        for hd in range(MLA_HEADS):
            v_ref[0, hd, blk] = (vf[:, hd * LANES:(hd + 1) * LANES] + ones_col).T.astype(BF16)
    ext_ref[0:POOL_HALO, :] = ext_ref[tm:tm + POOL_HALO, :]


def _even_front(x, pos_row, norm, w_in, pool_w, pool_scale, qn, wq, kvn, wk, wv, freq_col, ones_col,
                tm, tk):
    B, S, D = x.shape
    pool_dim = pool_scale.shape[1]
    const = lambda b, i: (0, 0)
    return pl.pallas_call(
        _even_front_kernel,
        grid=(B, S // tm),
        in_specs=[
            pl.BlockSpec((1, tm, D), lambda b, i: (b, i, 0)),
            pl.BlockSpec((1, 1, tm), lambda b, i: (b, 0, i)),
            pl.BlockSpec(norm.shape, const),
            pl.BlockSpec(w_in.shape, const),
            pl.BlockSpec(pool_w.shape, lambda b, i: (0, 0, 0)),
            pl.BlockSpec(pool_scale.shape, const),
            pl.BlockSpec(qn.shape, const),
            pl.BlockSpec(wq.shape, const),
            pl.BlockSpec(kvn.shape, const),
            pl.BlockSpec(wk.shape, const),
            pl.BlockSpec(wv.shape, const),
            pl.BlockSpec(freq_col.shape, const),
            pl.BlockSpec(ones_col.shape, const),
        ],
        out_specs=[
            pl.BlockSpec((1, tm, pool_dim), lambda b, i: (b, i, 0)),
            pl.BlockSpec((1, MLA_HEADS, LANES, tm), lambda b, i: (b, 0, 0, i)),
            pl.BlockSpec((1, MLA_HEADS, tm, LANES), lambda b, i: (b, 0, i, 0)),
            pl.BlockSpec((1, MLA_HEADS, tm // tk, LANES, tk), lambda b, i: (b, 0, i, 0, 0)),
        ],
        out_shape=[
            jax.ShapeDtypeStruct((B, S, pool_dim), BF16),
            jax.ShapeDtypeStruct((B, MLA_HEADS, LANES, S), BF16),
            jax.ShapeDtypeStruct((B, MLA_HEADS, S, LANES), BF16),
            jax.ShapeDtypeStruct((B, MLA_HEADS, S // tk, LANES, tk), BF16),
        ],
        scratch_shapes=[pltpu.VMEM((POOL_HALO + tm, pool_dim), F32)],
        compiler_params=_params("arbitrary", "arbitrary"),
        name="even_front",
    )(x, pos_row, norm, w_in, pool_w, pool_scale, qn, wq, kvn, wk, wv, freq_col, ones_col)


def _attn_kernel(qt_ref, qn_ref, k_ref, vt_ref, o_ref, m_ref, mt_ref, acc_ref, s_ref, *, tq, tk, hps):
    qi = pl.program_id(2)
    gsub = s_ref.shape[0]
    assert gsub * tk == 2 * tq
    m_ref[...] = jnp.full(m_ref.shape, NEG_INF, F32)
    acc_ref[...] = jnp.zeros(acc_ref.shape, F32)

    def stage(g_fin, n_fin, g_new, n_new, mask_off, q_ref=qt_ref):
        for hh in range(hps):
            if n_fin:
                m_old = m_ref[hh]
                m_new = jnp.maximum(m_old, mt_ref[hh])
                alpha = jnp.exp2(m_old - m_new)
            pv = None
            mt = None
            for u in range(max(n_fin, n_new)):
                if u < n_fin:
                    p = jnp.exp2(s_ref[u, hh] - m_new).astype(BF16)
                    d = _dot(vt_ref[0, hh, g_fin * gsub + u], p)
                    pv = d if pv is None else pv + d
                if u < n_new:
                    r0 = pl.multiple_of((g_new * gsub + u) * tk, tk)
                    s = _dot(k_ref[0, hh, pl.ds(r0, tk), :], q_ref[0, hh])
                    if mask_off is not None:
                        kk = lax.broadcasted_iota(jnp.int32, (tk, tq), 0) + (u * tk - mask_off)
                        qq = lax.broadcasted_iota(jnp.int32, (tk, tq), 1)
                        s = jnp.where(kk <= qq, s, NEG_INF)
                    s_ref[u, hh] = s
                    cm = jnp.max(s, axis=0, keepdims=True)
                    mt = cm if mt is None else jnp.maximum(mt, cm)
            if n_fin:
                acc_ref[hh] = alpha * acc_ref[hh] + pv
                m_ref[hh] = m_new
            if n_new:
                mt_ref[hh] = mt

    n_full = qi // 2
    odd = qi % 2 == 1
    even = jnp.logical_not(odd)
    some_full = n_full > 0
    first = qi == 0
    last = qi == pl.num_programs(2) - 1
    inner = jnp.logical_not(jnp.logical_or(first, last))
    half = gsub // 2

    @pl.when(first)
    def _():
        stage(None, 0, 0, half, 0)

    def body(t, carry):
        stage(2 * t, gsub, 2 * t + 1, gsub, None)
        stage(2 * t + 1, gsub, 2 * t + 2, gsub, None)
        return carry

    steady = jnp.maximum(n_full - 1, 0)
    lax.fori_loop(0, steady // 2, body, 0)

    @pl.when(steady % 2 == 1)
    def _():
        stage(steady - 1, gsub, steady, gsub, None)

    @pl.when(jnp.logical_and(some_full, odd))
    def _():
        stage(n_full - 1, gsub, n_full, gsub, tq)

    @pl.when(jnp.logical_and(some_full, even))
    def _():
        stage(n_full - 1, gsub, n_full, half, 0)

    for parity, n_fin in ((odd, gsub), (even, half)):
        @pl.when(jnp.logical_and(parity, last))
        def _():
            stage(n_full, n_fin, None, 0, None)

        @pl.when(jnp.logical_and(parity, inner))
        def _():
            stage(n_full, n_fin, 0, gsub, None, qn_ref)

    @pl.when(jnp.logical_and(first, jnp.logical_not(last)))
    def _():
        stage(n_full, half, 0, gsub, tq, qn_ref)

    for pr in range(hps // 2):
        ot = jnp.concatenate(
            [acc_ref[hh, :V_HEAD_DIM, :] / acc_ref[hh, V_HEAD_DIM:V_HEAD_DIM + 1, :]
             for hh in (2 * pr, 2 * pr + 1)], axis=0)
        o_ref[0, :, pr * LANES:(pr + 1) * LANES] = ot.T.astype(BF16)


def _attention(qt, k, vt, tq, hps):
    B, H, S, _ = k.shape
    tk = vt.shape[-1]
    resident = pl.Buffered(1)
    nq = S // tq
    return pl.pallas_call(
        functools.partial(_attn_kernel, tq=tq, tk=tk, hps=hps),
        grid=(B, H // hps, nq),
        in_specs=[
            pl.BlockSpec((1, hps, LANES, tq), lambda b, p, i: (b, p, 0, i)),
            pl.BlockSpec((1, hps, LANES, tq), lambda b, p, i: (b, p, 0, jnp.minimum(i + 1, nq - 1))),
            pl.BlockSpec((1, hps, S, LANES), lambda b, p, i: (b, p, 0, 0), pipeline_mode=resident),
            pl.BlockSpec((1, hps, S // tk, LANES, tk), lambda b, p, i: (b, p, 0, 0, 0),
                         pipeline_mode=resident),
        ],
        out_specs=pl.BlockSpec((1, tq, (hps // 2) * LANES), lambda b, p, i: (b, i, p)),
        out_shape=jax.ShapeDtypeStruct((B, S, (H // 2) * LANES), BF16),
        scratch_shapes=[pltpu.VMEM((hps, 1, tq), F32), pltpu.VMEM((hps, 1, tq), F32),
                        pltpu.VMEM((hps, LANES, tq), F32), pltpu.VMEM((2 * tq // tk, hps, tk, tq), F32)],
        compiler_params=_params("arbitrary", "arbitrary", "arbitrary"),
        name="mla_attention",
    )(qt, qt, k, vt)


def _gelu_tanh_of_half(hx):
    c = math.sqrt(2.0 / math.pi)
    inner = hx * ((8.0 * 0.044715 * c) * (hx * hx) + 2.0 * c)
    return hx * (1.0 + jnp.tanh(inner))


def _odd_kernel(x_ref, pos_ref, norm_ref, w_in_ref, conv_w_ref, conv_b_ref, wg_ref, bg_ref,
                lam_ref, w_out_ref, out_ref, ext_ref, hcar_ref):
    i = pl.program_id(1)
    tm = x_ref.shape[1]
    width = lam_ref.shape[1]
    hd = width // LRU_HEADS
    bm = min(ODD_BLOCK_ROWS, tm)
    nblk = tm // bm

    @pl.when(i == 0)
    def _():
        ext_ref[0:CONV_HALO, :] = jnp.zeros((CONV_HALO, width), F32)
        hcar_ref[...] = jnp.zeros(hcar_ref.shape, F32)

    nlam = -lam_ref[...]
    softplus = jnp.maximum(nlam, 0.0) + jnp.log1p(jnp.exp(-jnp.abs(nlam)))
    half_c = (-0.5 * LRU_C) * softplus
    sub = lax.broadcasted_iota(jnp.int32, (8, width), 0)
    keeps = {d: sub >= d for d in (1, 2, 4)}
    hprev = hcar_ref[...]
    def in_proj(blk):
        x = x_ref[0, blk * bm:(blk + 1) * bm, :]
        return x, _dot(_rms(x, norm_ref[...]).astype(BF16), w_in_ref[...])

    ahead = in_proj(0)
    for blk in range(nblk):
        r0 = blk * bm
        x, z = ahead
        if blk + 1 < nblk:
            ahead = in_proj(blk + 1)
        gate = z[:, :width]
        xb = z[:, width:]

        ext_ref[CONV_HALO + r0:CONV_HALO + r0 + bm, :] = xb
        xc = conv_b_ref[...] + conv_w_ref[CONV_WIDTH - 1:CONV_WIDTH, :] * xb
        for k in range(CONV_WIDTH - 1):
            off = CONV_HALO - (CONV_WIDTH - 1) + k + r0
            xc = xc + conv_w_ref[k:k + 1, :] * ext_ref[off:off + bm, :]

        xcb = xc.astype(BF16)
        r_parts, i_parts = [], []
        for h in range(LRU_HEADS):
            g = _dot(xcb[:, h * hd:(h + 1) * hd], wg_ref[h])
            r_parts.append(g[:, :hd])
            i_parts.append(g[:, hd:])
        tr = jnp.tanh(jnp.concatenate(r_parts, axis=-1) + bg_ref[0:1, :])
        ti = jnp.tanh(jnp.concatenate(i_parts, axis=-1) + bg_ref[1:2, :])
        log_a = tr * half_c + half_c
        ig = 0.5 * ti + 0.5
        a = jnp.exp(log_a)
        mult = jnp.sqrt(jnp.maximum(-jnp.tanh(log_a) * (a * a + 1.0), 0.0))
        reset = pos_ref[0, r0:r0 + bm, :] == 0
        a = jnp.where(reset, 0.0, a)
        b = jnp.where(reset, 1.0, mult) * (ig * xc)

        hs = []
        for c in range(bm // 8):
            at = a[c * 8:(c + 1) * 8, :]
            bt = b[c * 8:(c + 1) * 8, :]
            for d in (1, 2, 4):
                a_s = jnp.where(keeps[d], pltpu.roll(at, d, 0), 1.0)
                b_s = jnp.where(keeps[d], pltpu.roll(bt, d, 0), 0.0)
                bt = at * b_s + bt
                at = at * a_s
            hcur = at * hprev + bt
            hs.append(hcur)
            hprev = jnp.broadcast_to(hcur[7:8, :], (8, width))
        y = _gelu_tanh_of_half(gate) * jnp.concatenate(hs, axis=0)
        out_ref[0, r0:r0 + bm, :] = x + _dot(y.astype(BF16), w_out_ref[...])
    ext_ref[0:CONV_HALO, :] = ext_ref[tm:tm + CONV_HALO, :]
    hcar_ref[...] = hprev


def _odd_mixer(x, pos_col, norm, w_in, conv_w, conv_b, wg, bg, lam, w_out, tm):
    B, S, D = x.shape
    width = lam.shape[1]
    const = lambda b, i: (0, 0)
    return pl.pallas_call(
        _odd_kernel,
        grid=(B, S // tm),
        in_specs=[
            pl.BlockSpec((1, tm, D), lambda b, i: (b, i, 0)),
            pl.BlockSpec((1, tm, 1), lambda b, i: (b, i, 0)),
            pl.BlockSpec(norm.shape, const),
            pl.BlockSpec(w_in.shape, const),
            pl.BlockSpec(conv_w.shape, const),
            pl.BlockSpec(conv_b.shape, const),
            pl.BlockSpec(wg.shape, lambda b, i: (0, 0, 0)),
            pl.BlockSpec(bg.shape, const),
            pl.BlockSpec(lam.shape, const),
            pl.BlockSpec(w_out.shape, const),
        ],
        out_specs=pl.BlockSpec((1, tm, D), lambda b, i: (b, i, 0)),
        out_shape=jax.ShapeDtypeStruct((B, S, D), F32),
        scratch_shapes=[pltpu.VMEM((CONV_HALO + tm, width), F32), pltpu.VMEM((8, width), F32)],
        compiler_params=_params("arbitrary", "arbitrary"),
        name="odd_mixer",
    )(x, pos_col, norm, w_in, conv_w, conv_b, wg, bg, lam, w_out)


def _memkv_kernel(mem_ref, norm_ref, w_ref, out_ref):
    out_ref[0] = _dot(_rms(mem_ref[0], norm_ref[...]).astype(BF16), w_ref[...]).astype(BF16)


def _memkv(mem, norm, w_kv, layer):
    B, M, D = mem.shape
    n = w_kv.shape[2]
    return pl.pallas_call(
        _memkv_kernel,
        grid=(B,),
        in_specs=[pl.BlockSpec((1, M, D), lambda b: (b, 0, 0)),
                  pl.BlockSpec(norm.shape, lambda b: (0, 0)),
                  _layer_spec(w_kv, layer)],
        out_specs=pl.BlockSpec((1, M, n), lambda b: (b, 0, 0)),
        out_shape=jax.ShapeDtypeStruct((B, M, n), BF16),
        compiler_params=_params("arbitrary"),
        name="mem_kv",
    )(mem, norm, w_kv)


def _xattn_kernel(*refs, with_mix):
    if with_mix:
        x_ref, yp_ref, ya_ref, wop_ref, woa_ref, norm_ref, wq_ref, kv_ref, wo_ref, out_ref = refs
        x = x_ref[...] + _dot(yp_ref[...], wop_ref[...]) + _dot(ya_ref[...], woa_ref[...])
    else:
        x_ref, norm_ref, wq_ref, kv_ref, wo_ref, out_ref = refs
        x = x_ref[...]
    d = x.shape[1]
    hd = d // MEM_HEADS
    q = _dot(_rms(x, norm_ref[...]).astype(BF16), wq_ref[...]).astype(BF16)
    outs = []
    for h in range(MEM_HEADS):
        kh = kv_ref[0, :, h * hd:(h + 1) * hd]
        vh = kv_ref[0, :, d + h * hd:d + (h + 1) * hd]
        s = _dot_nt(q[:, h * hd:(h + 1) * hd], kh) * (hd ** -0.5)
        p = jnp.exp(s - jnp.max(s, axis=-1, keepdims=True))
        o = _dot(p.astype(BF16), vh) / jnp.sum(p, axis=-1, keepdims=True)
        outs.append(o.astype(BF16))
    out_ref[...] = x + _dot(jnp.concatenate(outs, axis=-1), wo_ref[...])


def _xattn(x2d, mix, norm, wq, kv, wo, layer, tm, tiles_per_batch):
    T, D = x2d.shape
    const = lambda i: (0, 0)
    row = lambda i: (i, 0)
    args = [x2d]
    specs = [pl.BlockSpec((tm, D), row)]
    if mix is not None:
        yp, ya, wop, woa = mix
        args += [yp, ya, wop, woa]
        specs += [pl.BlockSpec((tm, yp.shape[1]), row), pl.BlockSpec((tm, ya.shape[1]), row),
                  pl.BlockSpec(wop.shape, const), pl.BlockSpec(woa.shape, const)]
    args += [norm, wq, kv, wo]
    specs += [pl.BlockSpec(norm.shape, const), _layer_spec(wq, layer),
              pl.BlockSpec((1,) + kv.shape[1:], lambda i: (i // tiles_per_batch, 0, 0)),
              _layer_spec(wo, layer)]
    return pl.pallas_call(
        functools.partial(_xattn_kernel, with_mix=mix is not None),
        grid=(T // tm,),
        in_specs=specs,
        out_specs=pl.BlockSpec((tm, D), row),
        out_shape=jax.ShapeDtypeStruct((T, D), F32),
        compiler_params=_params("arbitrary"),
        name="mem_xattn_mix" if mix is not None else "mem_xattn",
    )(*args)


def _ffn_kernel(x_ref, norm_ref, wgu_ref, wd_ref, fnorm_ref, out_ref, *, final):
    ff = wd_ref.shape[0]
    x = x_ref[...]
    h = _rms(x, norm_ref[...]).astype(BF16)
    g = _dot(h, wgu_ref[:, :ff])
    u = _dot(h, wgu_ref[:, ff:])
    act = (g * jax.nn.sigmoid(g) * u).astype(BF16)
    y = x + _dot(act, wd_ref[...])
    out_ref[...] = _rms(y, fnorm_ref[...]) if final else y


def _layer_spec(w_all, layer, **kw):
    zeros = (0,) * (w_all.ndim - 1)
    return pl.BlockSpec((None,) + w_all.shape[1:], lambda *_: (layer,) + zeros, **kw)


def _ffn(x2d, norm, w_gate_up, w_down, layer, fnorm, tm, final):
    T, D = x2d.shape
    const = lambda i: (0, 0)
    resident = pl.Buffered(1)
    return pl.pallas_call(
        functools.partial(_ffn_kernel, final=final),
        grid=(T // tm,),
        in_specs=[
            pl.BlockSpec((tm, D), lambda i: (i, 0)),
            pl.BlockSpec(norm.shape, const),
            _layer_spec(w_gate_up, layer, pipeline_mode=resident),
            _layer_spec(w_down, layer, pipeline_mode=resident),
            pl.BlockSpec(fnorm.shape, const),
        ],
        out_specs=pl.BlockSpec((tm, D), lambda i: (i, 0)),
        out_shape=jax.ShapeDtypeStruct((T, D), F32),
        compiler_params=_params("arbitrary"),
        name="ffn_final" if final else "ffn",
    )(x2d, norm, w_gate_up, w_down, fnorm)


def _tiles(S):
    tm = min(512, S)
    tq = tm
    tk = min(256, tq)
    return tm, tq, tk


def _pad_cols(w, groups, width):
    k = w.shape[0]
    w = w.reshape(k, groups, -1)
    return jnp.pad(w, ((0, 0), (0, 0), (0, width - w.shape[2]))).reshape(k, groups * width)


def kernel(x, mem, positions, ev_norm, ev_w_in, ev_pool_w, ev_pool_scale, ev_q_norm, ev_w_q_up, ev_kv_norm, ev_w_kv_up, ev_w_out, od_norm, od_w_in, od_conv_w, od_conv_b, od_w_rgate, od_b_rgate, od_w_igate, od_b_igate, od_lambda, od_w_out, xa_norm_x, xa_norm_mem, xa_w_q, xa_w_kv, xa_w_o, ffn_norm, ffn_w_gate_up, ffn_w_down, final_norm):
    B, S, D = x.shape
    depth = xa_w_q.shape[0]
    ff = ffn_w_down.shape[1]
    tm, tq, tk = _tiles(S)
    pos_col = positions.reshape(B, S, 1)
    pool_dim = ev_pool_scale.shape[1]
    q_rank = ev_q_norm.shape[1]
    kv_rank = ev_kv_norm.shape[1]

    pos_row = positions.reshape(B, 1, S)
    inv_freq = ROPE_BASE ** (-jnp.arange(0, QK_ROPE_DIM, 2, dtype=F32) / QK_ROPE_DIM)
    freq_col = inv_freq.reshape(-1, 1)
    ones_col = jnp.zeros((1, LANES), F32).at[0, V_HEAD_DIM].set(1.0)

    row = lambda v: v.reshape(1, -1)
    xa_wq, xa_wkv, xa_wo = (w.astype(BF16) for w in (xa_w_q, xa_w_kv, xa_w_o))
    ffn_wgu, ffn_wd = ffn_w_gate_up.astype(BF16), ffn_w_down.astype(BF16)
    h = x
    for layer in range(depth):
        j = layer // 2
        if layer % 2 == 0:
            lat = pool_dim + q_rank + kv_rank
            w_in = jnp.concatenate(
                [ev_w_in[j][:, :lat], jnp.zeros((D, QK_NOPE_DIM), F32), ev_w_in[j][:, lat:],
                 jnp.zeros((D, LANES - QK_DIM), F32)], axis=1).astype(BF16)
            wq = _pad_cols(ev_w_q_up[j], MLA_HEADS, LANES).astype(BF16)
            wkv = ev_w_kv_up[j].reshape(kv_rank, MLA_HEADS, QK_NOPE_DIM + V_HEAD_DIM)
            wk = _pad_cols(wkv[:, :, :QK_NOPE_DIM].reshape(kv_rank, -1), MLA_HEADS, LANES).astype(BF16)
            wv = _pad_cols(wkv[:, :, QK_NOPE_DIM:].reshape(kv_rank, -1), MLA_HEADS, LANES).astype(BF16)
            y_pool, q, k, v = _even_front(
                h, pos_row, row(ev_norm[j]), w_in, ev_pool_w[j].astype(BF16), row(ev_pool_scale[j]),
                row(ev_q_norm[j]), wq, row(ev_kv_norm[j]), wk, wv, freq_col, ones_col, tm, tk)
            y_att = _attention(q, k, v, tq, ATTN_HEADS_PER_STEP)
            w_out = ev_w_out[j].astype(BF16)
            mix = (y_pool.reshape(B * S, -1), y_att.reshape(B * S, -1),
                   w_out[:pool_dim], w_out[pool_dim:])
            h2d = h.reshape(B * S, D)
        else:
            lw = od_lambda.shape[1]
            hd = lw // LRU_HEADS
            wg = (0.5 * jnp.concatenate([od_w_rgate[j], od_w_igate[j]], axis=-1)).astype(BF16)
            bg = 0.5 * jnp.stack([od_b_rgate[j], od_b_igate[j]])
            col_scale = jnp.where(jnp.arange(2 * lw) < lw, 0.5, 1.0).astype(F32)
            h = _odd_mixer(h, pos_col, row(od_norm[j]), (od_w_in[j] * col_scale).astype(BF16), od_conv_w[j],
                           row(od_conv_b[j]), wg, bg, row(od_lambda[j]), od_w_out[j].astype(BF16), tm)
            mix = None
            h2d = h.reshape(B * S, D)
        kv = _memkv(mem, row(xa_norm_mem[layer]), xa_wkv, layer)
        h2d = _xattn(h2d, mix, row(xa_norm_x[layer]), xa_wq, kv, xa_wo, layer, tm, S // tm)
        h2d = _ffn(h2d, row(ffn_norm[layer]), ffn_wgu, ffn_wd, layer, row(final_norm), tm,
                   final=layer == depth - 1)
        h = h2d.reshape(B, S, D)
    return h
```

```python
import functools
import math

import jax
import jax.numpy as jnp
from jax import lax
from jax.experimental import pallas as pl
from jax.experimental.pallas import tpu as pltpu

F32 = jnp.float32
BF16 = jnp.bfloat16

LANES = 128

POOL_WINDOWS = (2, 4, 8, 16)
POOL_HALO = 16
MLA_HEADS = 8
QK_NOPE_DIM = 64
QK_ROPE_DIM = 32
QK_DIM = QK_NOPE_DIM + QK_ROPE_DIM
V_HEAD_DIM = 64
ATTN_HEADS_PER_STEP = 4
QK_EXP2_SCALE = (QK_DIM ** -0.5) * math.log2(math.e)
ROPE_BASE = 10000.0
LRU_HEADS = 4
CONV_WIDTH = 4
CONV_HALO = 8
ODD_BLOCK_ROWS = 256
LRU_C = 8.0
MEM_HEADS = 4
RMS_EPS = 1e-6
NEG_INF = -1e30

VMEM_LIMIT_BYTES = 56 * 1024 * 1024


def _params(*sem):
    return pltpu.CompilerParams(dimension_semantics=sem, vmem_limit_bytes=VMEM_LIMIT_BYTES)


def _rms(x, g):
    return x * lax.rsqrt(jnp.mean(x * x, axis=-1, keepdims=True) + RMS_EPS) * g


def _dot(a, b):
    return jnp.dot(a, b, preferred_element_type=F32)


def _dot_nt(a, b):
    return lax.dot_general(a, b, (((1,), (1,)), ((), ())), preferred_element_type=F32)


def _even_front_kernel(x_ref, pos_ref, norm_ref, w_in_ref, pool_w_ref, pool_scale_ref,
                       qn_ref, wq_ref, kvn_ref, wk_ref, wv_ref, freq_ref, ones_ref,
                       ypool_ref, q_ref, k_ref, v_ref, ext_ref):
    i = pl.program_id(1)
    tm = x_ref.shape[1]
    pool_dim = ypool_ref.shape[2]
    bm = v_ref.shape[4]
    half = QK_ROPE_DIM // 2
    r1, r2, r3 = QK_NOPE_DIM, QK_NOPE_DIM + half, QK_DIM
    kv0 = pool_dim + qn_ref.shape[1]
    ones_col = ones_ref[...]

    @pl.when(i == 0)
    def _():
        ext_ref[0:POOL_HALO, :] = jnp.zeros((POOL_HALO, pool_dim), F32)

    def in_proj(blk):
        h = _rms(x_ref[0, blk * bm:(blk + 1) * bm, :], norm_ref[...]).astype(BF16)
        return _dot(h, w_in_ref[...])

    ahead = in_proj(0)
    for blk in range(tm // bm):
        r0 = blk * bm
        z = ahead
        if (blk + 1) * bm < tm:
            ahead = in_proj(blk + 1)

        u = z[:, :pool_dim]
        ext_ref[POOL_HALO + r0:POOL_HALO + r0 + bm, :] = u
        t = i * tm + r0 + lax.broadcasted_iota(jnp.int32, (bm, 1), 0)
        parts = []
        for g, w in enumerate(POOL_WINDOWS):
            cols = slice(g * LANES, (g + 1) * LANES)
            acc = ext_ref[r0:r0 + POOL_HALO + bm, cols]
            d = 1
            while d < w:
                acc = acc + pltpu.roll(acc, d, 0)
                d *= 2
            cnt = jnp.minimum(t + 1, w).astype(F32)
            pooled = acc[POOL_HALO:, :] / cnt - u[:, cols]
            parts.append(_dot(pooled.astype(BF16), pool_w_ref[g]))
        y_pool = jnp.concatenate(parts, axis=-1) * pool_scale_ref[...]
        ypool_ref[0, r0:r0 + bm, :] = y_pool.astype(BF16)

        ang = freq_ref[...] * pos_ref[0, :, r0:r0 + bm].astype(F32)
        cos = jnp.cos(ang)
        sin = jnp.sin(ang)

        def rope_t(xt):
            t1, t2 = xt[r1:r2, :], xt[r2:r3, :]
            return jnp.concatenate(
                [xt[:r1, :], t1 * cos - t2 * sin, t2 * cos + t1 * sin, xt[r3:, :]], axis=0)

        qf = _dot(_rms(z[:, pool_dim:kv0], qn_ref[...]).astype(BF16), wq_ref[...])
        for hd in range(MLA_HEADS):
            qt = rope_t(qf[:, hd * LANES:(hd + 1) * LANES].T)
            q_ref[0, hd, :, r0:r0 + bm] = (qt * QK_EXP2_SCALE).astype(BF16)

        kv_lat = _rms(z[:, kv0:kv0 + kvn_ref.shape[1]], kvn_ref[...]).astype(BF16)
        kf = _dot(kv_lat, wk_ref[...])
        vf = _dot(kv_lat, wv_ref[...])
        k_rope = rope_t(z[:, kv0 + kvn_ref.shape[1]:].T).T
        for hd in range(MLA_HEADS):
            k_ref[0, hd, r0:r0 + bm, :] = (kf[:, hd * LANES:(hd + 1) * LANES] + k_rope).astype(BF16)
        for hd in range(MLA_HEADS):
            v_ref[0, hd, blk] = (vf[:, hd * LANES:(hd + 1) * LANES] + ones_col).T.astype(BF16)
    ext_ref[0:POOL_HALO, :] = ext_ref[tm:tm + POOL_HALO, :]


def _even_front(x, pos_row, norm, w_in, pool_w, pool_scale, qn, wq, kvn, wk, wv, freq_col, ones_col,
                tm, tk):
    B, S, D = x.shape
    pool_dim = pool_scale.shape[1]
    const = lambda b, i: (0, 0)
    return pl.pallas_call(
        _even_front_kernel,
        grid=(B, S // tm),
        in_specs=[
            pl.BlockSpec((1, tm, D), lambda b, i: (b, i, 0)),
            pl.BlockSpec((1, 1, tm), lambda b, i: (b, 0, i)),
            pl.BlockSpec(norm.shape, const),
            pl.BlockSpec(w_in.shape, const),
            pl.BlockSpec(pool_w.shape, lambda b, i: (0, 0, 0)),
            pl.BlockSpec(pool_scale.shape, const),
            pl.BlockSpec(qn.shape, const),
            pl.BlockSpec(wq.shape, const),
            pl.BlockSpec(kvn.shape, const),
            pl.BlockSpec(wk.shape, const),
            pl.BlockSpec(wv.shape, const),
            pl.BlockSpec(freq_col.shape, const),
            pl.BlockSpec(ones_col.shape, const),
        ],
        out_specs=[
            pl.BlockSpec((1, tm, pool_dim), lambda b, i: (b, i, 0)),
            pl.BlockSpec((1, MLA_HEADS, LANES, tm), lambda b, i: (b, 0, 0, i)),
            pl.BlockSpec((1, MLA_HEADS, tm, LANES), lambda b, i: (b, 0, i, 0)),
            pl.BlockSpec((1, MLA_HEADS, tm // tk, LANES, tk), lambda b, i: (b, 0, i, 0, 0)),
        ],
        out_shape=[
            jax.ShapeDtypeStruct((B, S, pool_dim), BF16),
            jax.ShapeDtypeStruct((B, MLA_HEADS, LANES, S), BF16),
            jax.ShapeDtypeStruct((B, MLA_HEADS, S, LANES), BF16),
            jax.ShapeDtypeStruct((B, MLA_HEADS, S // tk, LANES, tk), BF16),
        ],
        scratch_shapes=[pltpu.VMEM((POOL_HALO + tm, pool_dim), F32)],
        compiler_params=_params("arbitrary", "arbitrary"),
        name="even_front",
    )(x, pos_row, norm, w_in, pool_w, pool_scale, qn, wq, kvn, wk, wv, freq_col, ones_col)


def _attn_kernel(qt_ref, qn_ref, k_ref, vt_ref, o_ref, m_ref, mt_ref, acc_ref, s_ref, *, tq, tk, hps):
    qi = pl.program_id(2)
    gsub = s_ref.shape[0]
    assert gsub * tk == 2 * tq
    m_ref[...] = jnp.full(m_ref.shape, NEG_INF, F32)
    acc_ref[...] = jnp.zeros(acc_ref.shape, F32)

    def stage(g_fin, n_fin, g_new, n_new, mask_off, q_ref=qt_ref):
        for hh in range(hps):
            if n_fin:
                m_old = m_ref[hh]
                m_new = jnp.maximum(m_old, mt_ref[hh])
                alpha = jnp.exp2(m_old - m_new)
            pv = None
            mt = None
            for u in range(max(n_fin, n_new)):
                if u < n_fin:
                    p = jnp.exp2(s_ref[u, hh] - m_new).astype(BF16)
                    d = _dot(vt_ref[0, hh, g_fin * gsub + u], p)
                    pv = d if pv is None else pv + d
                if u < n_new:
                    r0 = pl.multiple_of((g_new * gsub + u) * tk, tk)
                    s = _dot(k_ref[0, hh, pl.ds(r0, tk), :], q_ref[0, hh])
                    if mask_off is not None:
                        kk = lax.broadcasted_iota(jnp.int32, (tk, tq), 0) + (u * tk - mask_off)
                        qq = lax.broadcasted_iota(jnp.int32, (tk, tq), 1)
                        s = jnp.where(kk <= qq, s, NEG_INF)
                    s_ref[u, hh] = s
                    cm = jnp.max(s, axis=0, keepdims=True)
                    mt = cm if mt is None else jnp.maximum(mt, cm)
            if n_fin:
                acc_ref[hh] = alpha * acc_ref[hh] + pv
                m_ref[hh] = m_new
            if n_new:
                mt_ref[hh] = mt

    n_full = qi // 2
    odd = qi % 2 == 1
    even = jnp.logical_not(odd)
    some_full = n_full > 0
    first = qi == 0
    last = qi == pl.num_programs(2) - 1
    inner = jnp.logical_not(jnp.logical_or(first, last))
    half = gsub // 2

    @pl.when(first)
    def _():
        stage(None, 0, 0, half, 0)

    def body(t, carry):
        stage(2 * t, gsub, 2 * t + 1, gsub, None)
        stage(2 * t + 1, gsub, 2 * t + 2, gsub, None)
        return carry

    steady = jnp.maximum(n_full - 1, 0)
    lax.fori_loop(0, steady // 2, body, 0)

    @pl.when(steady % 2 == 1)
    def _():
        stage(steady - 1, gsub, steady, gsub, None)

    @pl.when(jnp.logical_and(some_full, odd))
    def _():
        stage(n_full - 1, gsub, n_full, gsub, tq)

    @pl.when(jnp.logical_and(some_full, even))
    def _():
        stage(n_full - 1, gsub, n_full, half, 0)

    for parity, n_fin in ((odd, gsub), (even, half)):
        @pl.when(jnp.logical_and(parity, last))
        def _():
            stage(n_full, n_fin, None, 0, None)

        @pl.when(jnp.logical_and(parity, inner))
        def _():
            stage(n_full, n_fin, 0, gsub, None, qn_ref)

    @pl.when(jnp.logical_and(first, jnp.logical_not(last)))
    def _():
        stage(n_full, half, 0, gsub, tq, qn_ref)

    for pr in range(hps // 2):
        ot = jnp.concatenate(
            [acc_ref[hh, :V_HEAD_DIM, :] / acc_ref[hh, V_HEAD_DIM:V_HEAD_DIM + 1, :]
             for hh in (2 * pr, 2 * pr + 1)], axis=0)
        o_ref[0, :, pr * LANES:(pr + 1) * LANES] = ot.T.astype(BF16)


def _attention(qt, k, vt, tq, hps):
    B, H, S, _ = k.shape
    tk = vt.shape[-1]
    resident = pl.Buffered(1)
    nq = S // tq
    return pl.pallas_call(
        functools.partial(_attn_kernel, tq=tq, tk=tk, hps=hps),
        grid=(B, H // hps, nq),
        in_specs=[
            pl.BlockSpec((1, hps, LANES, tq), lambda b, p, i: (b, p, 0, i)),
            pl.BlockSpec((1, hps, LANES, tq), lambda b, p, i: (b, p, 0, jnp.minimum(i + 1, nq - 1))),
            pl.BlockSpec((1, hps, S, LANES), lambda b, p, i: (b, p, 0, 0), pipeline_mode=resident),
            pl.BlockSpec((1, hps, S // tk, LANES, tk), lambda b, p, i: (b, p, 0, 0, 0),
                         pipeline_mode=resident),
        ],
        out_specs=pl.BlockSpec((1, tq, (hps // 2) * LANES), lambda b, p, i: (b, i, p)),
        out_shape=jax.ShapeDtypeStruct((B, S, (H // 2) * LANES), BF16),
        scratch_shapes=[pltpu.VMEM((hps, 1, tq), F32), pltpu.VMEM((hps, 1, tq), F32),
                        pltpu.VMEM((hps, LANES, tq), F32), pltpu.VMEM((2 * tq // tk, hps, tk, tq), F32)],
        compiler_params=_params("arbitrary", "arbitrary", "arbitrary"),
        name="mla_attention",
    )(qt, qt, k, vt)


def _gelu_tanh_of_half(hx):
    c = math.sqrt(2.0 / math.pi)
    inner = hx * ((8.0 * 0.044715 * c) * (hx * hx) + 2.0 * c)
    return hx * (1.0 + jnp.tanh(inner))


def _odd_kernel(x_ref, pos_ref, norm_ref, w_in_ref, conv_w_ref, conv_b_ref, wg_ref, bg_ref,
                lam_ref, w_out_ref, out_ref, ext_ref, hcar_ref):
    i = pl.program_id(1)
    tm = x_ref.shape[1]
    width = lam_ref.shape[1]
    hd = width // LRU_HEADS
    bm = min(ODD_BLOCK_ROWS, tm)
    nblk = tm // bm

    @pl.when(i == 0)
    def _():
        ext_ref[0:CONV_HALO, :] = jnp.zeros((CONV_HALO, width), F32)
        hcar_ref[...] = jnp.zeros(hcar_ref.shape, F32)

    nlam = -lam_ref[...]
    softplus = jnp.maximum(nlam, 0.0) + jnp.log1p(jnp.exp(-jnp.abs(nlam)))
    half_c = (-0.5 * LRU_C) * softplus
    sub = lax.broadcasted_iota(jnp.int32, (8, width), 0)
    keeps = {d: sub >= d for d in (1, 2, 4)}
    hprev = hcar_ref[...]
    def in_proj(blk):
        x = x_ref[0, blk * bm:(blk + 1) * bm, :]
        return x, _dot(_rms(x, norm_ref[...]).astype(BF16), w_in_ref[...])

    ahead = in_proj(0)
    for blk in range(nblk):
        r0 = blk * bm
        x, z = ahead
        if blk + 1 < nblk:
            ahead = in_proj(blk + 1)
        gate = z[:, :width]
        xb = z[:, width:]

        ext_ref[CONV_HALO + r0:CONV_HALO + r0 + bm, :] = xb
        xc = conv_b_ref[...] + conv_w_ref[CONV_WIDTH - 1:CONV_WIDTH, :] * xb
        for k in range(CONV_WIDTH - 1):
            off = CONV_HALO - (CONV_WIDTH - 1) + k + r0
            xc = xc + conv_w_ref[k:k + 1, :] * ext_ref[off:off + bm, :]

        xcb = xc.astype(BF16)
        r_parts, i_parts = [], []
        for h in range(LRU_HEADS):
            g = _dot(xcb[:, h * hd:(h + 1) * hd], wg_ref[h])
            r_parts.append(g[:, :hd])
            i_parts.append(g[:, hd:])
        tr = jnp.tanh(jnp.concatenate(r_parts, axis=-1) + bg_ref[0:1, :])
        ti = jnp.tanh(jnp.concatenate(i_parts, axis=-1) + bg_ref[1:2, :])
        log_a = tr * half_c + half_c
        ig = 0.5 * ti + 0.5
        a = jnp.exp(log_a)
        mult = jnp.sqrt(jnp.maximum(-jnp.tanh(log_a) * (a * a + 1.0), 0.0))
        reset = pos_ref[0, r0:r0 + bm, :] == 0
        a = jnp.where(reset, 0.0, a)
        b = jnp.where(reset, 1.0, mult) * (ig * xc)

        hs = []
        for c in range(bm // 8):
            at = a[c * 8:(c + 1) * 8, :]
            bt = b[c * 8:(c + 1) * 8, :]
            for d in (1, 2, 4):
                a_s = jnp.where(keeps[d], pltpu.roll(at, d, 0), 1.0)
                b_s = jnp.where(keeps[d], pltpu.roll(bt, d, 0), 0.0)
                bt = at * b_s + bt
                at = at * a_s
            hcur = at * hprev + bt
            hs.append(hcur)
            hprev = jnp.broadcast_to(hcur[7:8, :], (8, width))
        y = _gelu_tanh_of_half(gate) * jnp.concatenate(hs, axis=0)
        out_ref[0, r0:r0 + bm, :] = x + _dot(y.astype(BF16), w_out_ref[...])
    ext_ref[0:CONV_HALO, :] = ext_ref[tm:tm + CONV_HALO, :]
    hcar_ref[...] = hprev


def _odd_mixer(x, pos_col, norm, w_in, conv_w, conv_b, wg, bg, lam, w_out, tm):
    B, S, D = x.shape
    width = lam.shape[1]
    const = lambda b, i: (0, 0)
    return pl.pallas_call(
        _odd_kernel,
        grid=(B, S // tm),
        in_specs=[
            pl.BlockSpec((1, tm, D), lambda b, i: (b, i, 0)),
            pl.BlockSpec((1, tm, 1), lambda b, i: (b, i, 0)),
            pl.BlockSpec(norm.shape, const),
            pl.BlockSpec(w_in.shape, const),
            pl.BlockSpec(conv_w.shape, const),
            pl.BlockSpec(conv_b.shape, const),
            pl.BlockSpec(wg.shape, lambda b, i: (0, 0, 0)),
            pl.BlockSpec(bg.shape, const),
            pl.BlockSpec(lam.shape, const),
            pl.BlockSpec(w_out.shape, const),
        ],
        out_specs=pl.BlockSpec((1, tm, D), lambda b, i: (b, i, 0)),
        out_shape=jax.ShapeDtypeStruct((B, S, D), F32),
        scratch_shapes=[pltpu.VMEM((CONV_HALO + tm, width), F32), pltpu.VMEM((8, width), F32)],
        compiler_params=_params("arbitrary", "arbitrary"),
        name="odd_mixer",
    )(x, pos_col, norm, w_in, conv_w, conv_b, wg, bg, lam, w_out)


def _memkv_kernel(mem_ref, norm_ref, w_ref, out_ref):
    out_ref[0] = _dot(_rms(mem_ref[0], norm_ref[...]).astype(BF16), w_ref[...]).astype(BF16)


def _memkv(mem, norm, w_kv, layer):
    B, M, D = mem.shape
    n = w_kv.shape[2]
    return pl.pallas_call(
        _memkv_kernel,
        grid=(B,),
        in_specs=[pl.BlockSpec((1, M, D), lambda b: (b, 0, 0)),
                  pl.BlockSpec(norm.shape, lambda b: (0, 0)),
                  _layer_spec(w_kv, layer)],
        out_specs=pl.BlockSpec((1, M, n), lambda b: (b, 0, 0)),
        out_shape=jax.ShapeDtypeStruct((B, M, n), BF16),
        compiler_params=_params("arbitrary"),
        name="mem_kv",
    )(mem, norm, w_kv)


def _layer_spec(w_all, layer, **kw):
    zeros = (0,) * (w_all.ndim - 1)
    return pl.BlockSpec((None,) + w_all.shape[1:], lambda *_: (layer,) + zeros, **kw)


def _post_kernel(*refs, with_mix, final):
    if with_mix:
        x_ref, yp_ref, ya_ref, wop_ref, woa_ref = refs[:5]
        x = x_ref[...] + _dot(yp_ref[...], wop_ref[...]) + _dot(ya_ref[...], woa_ref[...])
        refs = refs[5:]
    else:
        x = refs[0][...]
        refs = refs[1:]
    xnorm_ref, wq_ref, kv_ref, wo_ref, fnorm_ref, wgu_ref, wd_ref, onorm_ref, out_ref = refs
    d = x.shape[1]
    hd = d // MEM_HEADS
    q = _dot(_rms(x, xnorm_ref[...]).astype(BF16), wq_ref[...]).astype(BF16)
    outs = []
    for h in range(MEM_HEADS):
        kh = kv_ref[0, :, h * hd:(h + 1) * hd]
        vh = kv_ref[0, :, d + h * hd:d + (h + 1) * hd]
        s = _dot_nt(q[:, h * hd:(h + 1) * hd], kh) * (hd ** -0.5)
        p = jnp.exp(s - jnp.max(s, axis=-1, keepdims=True))
        o = _dot(p.astype(BF16), vh) / jnp.sum(p, axis=-1, keepdims=True)
        outs.append(o.astype(BF16))
    x = x + _dot(jnp.concatenate(outs, axis=-1), wo_ref[...])

    ff = wd_ref.shape[0]
    hn = _rms(x, fnorm_ref[...]).astype(BF16)
    g = _dot(hn, wgu_ref[:, :ff])
    u = _dot(hn, wgu_ref[:, ff:])
    act = (g * jax.nn.sigmoid(g) * u).astype(BF16)
    y = x + _dot(act, wd_ref[...])
    out_ref[...] = _rms(y, onorm_ref[...]) if final else y


def _post(x2d, mix, xnorm, wq, kv, wo, fnorm, w_gate_up, w_down, onorm, layer, tm, tiles_per_batch,
          final):
    T, D = x2d.shape
    const = lambda i: (0, 0)
    row = lambda i: (i, 0)
    resident = pl.Buffered(1)
    args = [x2d]
    specs = [pl.BlockSpec((tm, D), row)]
    if mix is not None:
        yp, ya, wop, woa = mix
        args += [yp, ya, wop, woa]
        specs += [pl.BlockSpec((tm, yp.shape[1]), row), pl.BlockSpec((tm, ya.shape[1]), row),
                  pl.BlockSpec(wop.shape, const, pipeline_mode=resident),
                  pl.BlockSpec(woa.shape, const, pipeline_mode=resident)]
    args += [xnorm, wq, kv, wo, fnorm, w_gate_up, w_down, onorm]
    specs += [pl.BlockSpec(xnorm.shape, const),
              _layer_spec(wq, layer, pipeline_mode=resident),
              pl.BlockSpec((1,) + kv.shape[1:], lambda i: (i // tiles_per_batch, 0, 0)),
              _layer_spec(wo, layer, pipeline_mode=resident),
              pl.BlockSpec(fnorm.shape, const),
              _layer_spec(w_gate_up, layer, pipeline_mode=resident),
              _layer_spec(w_down, layer, pipeline_mode=resident),
              pl.BlockSpec(onorm.shape, const)]
    return pl.pallas_call(
        functools.partial(_post_kernel, with_mix=mix is not None, final=final),
        grid=(T // tm,),
        in_specs=specs,
        out_specs=pl.BlockSpec((tm, D), row),
        out_shape=jax.ShapeDtypeStruct((T, D), F32),
        compiler_params=_params("arbitrary"),
        name="xattn_ffn_mix" if mix is not None else "xattn_ffn",
    )(*args)


def _tiles(S):
    tm = min(512, S)
    tq = tm
    tk = min(256, tq)
    tf = min(1024, S)
    return tm, tq, tk, tf


def _pad_cols(w, groups, width):
    k = w.shape[0]
    w = w.reshape(k, groups, -1)
    return jnp.pad(w, ((0, 0), (0, 0), (0, width - w.shape[2]))).reshape(k, groups * width)


def kernel(x, mem, positions, ev_norm, ev_w_in, ev_pool_w, ev_pool_scale, ev_q_norm, ev_w_q_up, ev_kv_norm, ev_w_kv_up, ev_w_out, od_norm, od_w_in, od_conv_w, od_conv_b, od_w_rgate, od_b_rgate, od_w_igate, od_b_igate, od_lambda, od_w_out, xa_norm_x, xa_norm_mem, xa_w_q, xa_w_kv, xa_w_o, ffn_norm, ffn_w_gate_up, ffn_w_down, final_norm):
    B, S, D = x.shape
    depth = xa_w_q.shape[0]
    tm, tq, tk, tf = _tiles(S)
    pos_col = positions.reshape(B, S, 1)
    pool_dim = ev_pool_scale.shape[1]
    q_rank = ev_q_norm.shape[1]
    kv_rank = ev_kv_norm.shape[1]

    pos_row = positions.reshape(B, 1, S)
    inv_freq = ROPE_BASE ** (-jnp.arange(0, QK_ROPE_DIM, 2, dtype=F32) / QK_ROPE_DIM)
    freq_col = inv_freq.reshape(-1, 1)
    ones_col = jnp.zeros((1, LANES), F32).at[0, V_HEAD_DIM].set(1.0)

    row = lambda v: v.reshape(1, -1)
    xa_wq, xa_wkv, xa_wo = (w.astype(BF16) for w in (xa_w_q, xa_w_kv, xa_w_o))
    ffn_wgu, ffn_wd = ffn_w_gate_up.astype(BF16), ffn_w_down.astype(BF16)
    h = x
    for layer in range(depth):
        j = layer // 2
        if layer % 2 == 0:
            lat = pool_dim + q_rank + kv_rank
            w_in = jnp.concatenate(
                [ev_w_in[j][:, :lat], jnp.zeros((D, QK_NOPE_DIM), F32), ev_w_in[j][:, lat:],
                 jnp.zeros((D, LANES - QK_DIM), F32)], axis=1).astype(BF16)
            wq = _pad_cols(ev_w_q_up[j], MLA_HEADS, LANES).astype(BF16)
            wkv = ev_w_kv_up[j].reshape(kv_rank, MLA_HEADS, QK_NOPE_DIM + V_HEAD_DIM)
            wk = _pad_cols(wkv[:, :, :QK_NOPE_DIM].reshape(kv_rank, -1), MLA_HEADS, LANES).astype(BF16)
            wv = _pad_cols(wkv[:, :, QK_NOPE_DIM:].reshape(kv_rank, -1), MLA_HEADS, LANES).astype(BF16)
            y_pool, q, k, v = _even_front(
                h, pos_row, row(ev_norm[j]), w_in, ev_pool_w[j].astype(BF16), row(ev_pool_scale[j]),
                row(ev_q_norm[j]), wq, row(ev_kv_norm[j]), wk, wv, freq_col, ones_col, tf, tk)
            y_att = _attention(q, k, v, tq, ATTN_HEADS_PER_STEP)
            w_out = ev_w_out[j].astype(BF16)
            mix = (y_pool.reshape(B * S, -1), y_att.reshape(B * S, -1),
                   w_out[:pool_dim], w_out[pool_dim:])
            h2d = h.reshape(B * S, D)
        else:
            lw = od_lambda.shape[1]
            wg = (0.5 * jnp.concatenate([od_w_rgate[j], od_w_igate[j]], axis=-1)).astype(BF16)
            bg = 0.5 * jnp.stack([od_b_rgate[j], od_b_igate[j]])
            col_scale = jnp.where(jnp.arange(2 * lw) < lw, 0.5, 1.0).astype(F32)
            h = _odd_mixer(h, pos_col, row(od_norm[j]), (od_w_in[j] * col_scale).astype(BF16), od_conv_w[j],
                           row(od_conv_b[j]), wg, bg, row(od_lambda[j]), od_w_out[j].astype(BF16), tf)
            mix = None
            h2d = h.reshape(B * S, D)
        kv = _memkv(mem, row(xa_norm_mem[layer]), xa_wkv, layer)
        h2d = _post(h2d, mix, row(xa_norm_x[layer]), xa_wq, kv, xa_wo, row(ffn_norm[layer]), ffn_wgu,
                    ffn_wd, row(final_norm), layer, tm, S // tm, final=layer == depth - 1)
        h = h2d.reshape(B, S, D)
    return h
```

```python
import functools
import math

import jax
import jax.numpy as jnp
from jax import lax
from jax.experimental import pallas as pl
from jax.experimental.pallas import tpu as pltpu

F32 = jnp.float32
BF16 = jnp.bfloat16

LANES = 128

POOL_WINDOWS = (2, 4, 8, 16)
POOL_HALO = 16
MLA_HEADS = 8
QK_NOPE_DIM = 64
QK_ROPE_DIM = 32
QK_DIM = QK_NOPE_DIM + QK_ROPE_DIM
V_HEAD_DIM = 64
ATTN_HEADS_PER_STEP = 4
ATTN_STAGES_PER_TRIP = 2
QK_EXP2_SCALE = (QK_DIM ** -0.5) * math.log2(math.e)
ROPE_BASE = 10000.0
LRU_HEADS = 4
CONV_WIDTH = 4
CONV_HALO = 8
ODD_BLOCK_ROWS = 256
LRU_C = 8.0
MEM_HEADS = 4
RMS_EPS = 1e-6
NEG_INF = -1e30

VMEM_LIMIT_BYTES = 56 * 1024 * 1024


def _params(*sem):
    return pltpu.CompilerParams(dimension_semantics=sem, vmem_limit_bytes=VMEM_LIMIT_BYTES)


def _rms(x, g):
    return x * lax.rsqrt(jnp.mean(x * x, axis=-1, keepdims=True) + RMS_EPS) * g


def _dot(a, b):
    return jnp.dot(a, b, preferred_element_type=F32)


def _dot_nt(a, b):
    return lax.dot_general(a, b, (((1,), (1,)), ((), ())), preferred_element_type=F32)


def _even_front_kernel(x_ref, pos_ref, norm_ref, w_in_ref, pool_w_ref, pool_scale_ref,
                       qn_ref, wq_ref, kvn_ref, wk_ref, wv_ref, freq_ref, ones_ref,
                       ypool_ref, q_ref, k_ref, v_ref, ext_ref):
    i = pl.program_id(1)
    tm = x_ref.shape[1]
    pool_dim = ypool_ref.shape[2]
    bm = v_ref.shape[4]
    half = QK_ROPE_DIM // 2
    r1, r2, r3 = QK_NOPE_DIM, QK_NOPE_DIM + half, QK_DIM
    kv0 = pool_dim + qn_ref.shape[1]
    ones_col = ones_ref[...]

    @pl.when(i == 0)
    def _():
        ext_ref[0:POOL_HALO, :] = jnp.zeros((POOL_HALO, pool_dim), F32)

    def in_proj(blk):
        h = _rms(x_ref[0, blk * bm:(blk + 1) * bm, :], norm_ref[...]).astype(BF16)
        return _dot(h, w_in_ref[...])

    ahead = in_proj(0)
    for blk in range(tm // bm):
        r0 = blk * bm
        z = ahead
        if (blk + 1) * bm < tm:
            ahead = in_proj(blk + 1)

        u = z[:, :pool_dim]
        ext_ref[POOL_HALO + r0:POOL_HALO + r0 + bm, :] = u
        t = i * tm + r0 + lax.broadcasted_iota(jnp.int32, (bm, 1), 0)
        parts = []
        for g, w in enumerate(POOL_WINDOWS):
            cols = slice(g * LANES, (g + 1) * LANES)
            acc = ext_ref[r0:r0 + POOL_HALO + bm, cols]
            d = 1
            while d < w:
                acc = acc + pltpu.roll(acc, d, 0)
                d *= 2
            cnt = jnp.minimum(t + 1, w).astype(F32)
            pooled = acc[POOL_HALO:, :] / cnt - u[:, cols]
            parts.append(_dot(pooled.astype(BF16), pool_w_ref[g]))
        y_pool = jnp.concatenate(parts, axis=-1) * pool_scale_ref[...]
        ypool_ref[0, r0:r0 + bm, :] = y_pool.astype(BF16)

        ang = freq_ref[...] * pos_ref[0, :, r0:r0 + bm].astype(F32)
        cos = jnp.cos(ang)
        sin = jnp.sin(ang)

        def rope_t(xt):
            t1, t2 = xt[r1:r2, :], xt[r2:r3, :]
            return jnp.concatenate(
                [xt[:r1, :], t1 * cos - t2 * sin, t2 * cos + t1 * sin, xt[r3:, :]], axis=0)

        qf = _dot(_rms(z[:, pool_dim:kv0], qn_ref[...]).astype(BF16), wq_ref[...])
        for hd in range(MLA_HEADS):
            qt = rope_t(qf[:, hd * LANES:(hd + 1) * LANES].T)
            q_ref[0, hd, :, r0:r0 + bm] = (qt * QK_EXP2_SCALE).astype(BF16)

        kv_lat = _rms(z[:, kv0:kv0 + kvn_ref.shape[1]], kvn_ref[...]).astype(BF16)
        kf = _dot(kv_lat, wk_ref[...])
        vf = _dot(kv_lat, wv_ref[...])
        k_rope = rope_t(z[:, kv0 + kvn_ref.shape[1]:].T).T
        for hd in range(MLA_HEADS):
            k_ref[0, hd, r0:r0 + bm, :] = (kf[:, hd * LANES:(hd + 1) * LANES] + k_rope).astype(BF16)
        for hd in range(MLA_HEADS):
            v_ref[0, hd, blk] = (vf[:, hd * LANES:(hd + 1) * LANES] + ones_col).T.astype(BF16)
    ext_ref[0:POOL_HALO, :] = ext_ref[tm:tm + POOL_HALO, :]


def _even_front(x, pos_row, norm, w_in, pool_w, pool_scale, qn, wq, kvn, wk, wv, freq_col, ones_col,
                tm, tk):
    B, S, D = x.shape
    pool_dim = pool_scale.shape[1]
    const = lambda b, i: (0, 0)
    return pl.pallas_call(
        _even_front_kernel,
        grid=(B, S // tm),
        in_specs=[
            pl.BlockSpec((1, tm, D), lambda b, i: (b, i, 0)),
            pl.BlockSpec((1, 1, tm), lambda b, i: (b, 0, i)),
            pl.BlockSpec(norm.shape, const),
            pl.BlockSpec(w_in.shape, const),
            pl.BlockSpec(pool_w.shape, lambda b, i: (0, 0, 0)),
            pl.BlockSpec(pool_scale.shape, const),
            pl.BlockSpec(qn.shape, const),
            pl.BlockSpec(wq.shape, const),
            pl.BlockSpec(kvn.shape, const),
            pl.BlockSpec(wk.shape, const),
            pl.BlockSpec(wv.shape, const),
            pl.BlockSpec(freq_col.shape, const),
            pl.BlockSpec(ones_col.shape, const),
        ],
        out_specs=[
            pl.BlockSpec((1, tm, pool_dim), lambda b, i: (b, i, 0)),
            pl.BlockSpec((1, MLA_HEADS, LANES, tm), lambda b, i: (b, 0, 0, i)),
            pl.BlockSpec((1, MLA_HEADS, tm, LANES), lambda b, i: (b, 0, i, 0)),
            pl.BlockSpec((1, MLA_HEADS, tm // tk, LANES, tk), lambda b, i: (b, 0, i, 0, 0)),
        ],
        out_shape=[
            jax.ShapeDtypeStruct((B, S, pool_dim), BF16),
            jax.ShapeDtypeStruct((B, MLA_HEADS, LANES, S), BF16),
            jax.ShapeDtypeStruct((B, MLA_HEADS, S, LANES), BF16),
            jax.ShapeDtypeStruct((B, MLA_HEADS, S // tk, LANES, tk), BF16),
        ],
        scratch_shapes=[pltpu.VMEM((POOL_HALO + tm, pool_dim), F32)],
        compiler_params=_params("arbitrary", "arbitrary"),
        name="even_front",
    )(x, pos_row, norm, w_in, pool_w, pool_scale, qn, wq, kvn, wk, wv, freq_col, ones_col)


def _attn_kernel(qt_ref, qn_ref, k_ref, vt_ref, o_ref, m_ref, mt_ref, acc_ref, s_ref, *, tq, tk, hps):
    qi = pl.program_id(2)
    gsub = s_ref.shape[0]
    assert gsub * tk == 2 * tq
    m_ref[...] = jnp.full(m_ref.shape, NEG_INF, F32)
    acc_ref[...] = jnp.zeros(acc_ref.shape, F32)

    hq = tq // 2

    def stage(g_fin, n_fin, g_new, n_new, mask_off, q_ref=qt_ref, fin_diag=False):
        for hh in range(hps):
            if n_fin:
                m_old = m_ref[hh]
                m_new = jnp.maximum(m_old, mt_ref[hh])
                alpha = jnp.exp2(m_old - m_new)
            pv = None
            mt = None
            for u in range(max(n_fin, n_new)):
                if u < n_fin:
                    vt = vt_ref[0, hh, g_fin * gsub + u]
                    if fin_diag and u == n_fin - 1:
                        p = jnp.exp2(s_ref[u, hh, :, hq:] - m_new[:, hq:]).astype(BF16)
                        d = jnp.concatenate([jnp.zeros((LANES, hq), F32), _dot(vt, p)], axis=1)
                    else:
                        d = _dot(vt, jnp.exp2(s_ref[u, hh] - m_new).astype(BF16))
                    pv = d if pv is None else pv + d
                if u < n_new:
                    r0 = pl.multiple_of((g_new * gsub + u) * tk, tk)
                    kt = k_ref[0, hh, pl.ds(r0, tk), :]
                    if mask_off is not None and u == n_new - 1:
                        assert u * tk - mask_off == hq
                        s = _dot(kt, q_ref[0, hh, :, hq:])
                        kk = lax.broadcasted_iota(jnp.int32, (tk, hq), 0)
                        qq = lax.broadcasted_iota(jnp.int32, (tk, hq), 1)
                        s = jnp.where(kk <= qq, s, NEG_INF)
                        s_ref[u, hh, :, hq:] = s
                        cm = jnp.concatenate([jnp.full((1, hq), NEG_INF, F32),
                                              jnp.max(s, axis=0, keepdims=True)], axis=1)
                    else:
                        s = _dot(kt, q_ref[0, hh])
                        if mask_off is not None:
                            kk = lax.broadcasted_iota(jnp.int32, (tk, tq), 0) + (u * tk - mask_off)
                            qq = lax.broadcasted_iota(jnp.int32, (tk, tq), 1)
                            s = jnp.where(kk <= qq, s, NEG_INF)
                        s_ref[u, hh] = s
                        cm = jnp.max(s, axis=0, keepdims=True)
                    mt = cm if mt is None else jnp.maximum(mt, cm)
            if n_fin:
                acc_ref[hh] = alpha * acc_ref[hh] + pv
                m_ref[hh] = m_new
            if n_new:
                mt_ref[hh] = mt

    n_full = qi // 2
    odd = qi % 2 == 1
    even = jnp.logical_not(odd)
    some_full = n_full > 0
    first = qi == 0
    last = qi == pl.num_programs(2) - 1
    inner = jnp.logical_not(jnp.logical_or(first, last))
    half = gsub // 2

    @pl.when(first)
    def _():
        stage(None, 0, 0, half, 0)

    def body(t, carry):
        for k in range(ATTN_STAGES_PER_TRIP):
            g = ATTN_STAGES_PER_TRIP * t + k
            stage(g, gsub, g + 1, gsub, None)
        return carry

    def rest(g, carry):
        stage(g, gsub, g + 1, gsub, None)
        return carry

    steady = jnp.maximum(n_full - 1, 0)
    trips = steady // ATTN_STAGES_PER_TRIP
    lax.fori_loop(0, trips, body, 0)
    lax.fori_loop(ATTN_STAGES_PER_TRIP * trips, steady, rest, 0)

    @pl.when(jnp.logical_and(some_full, odd))
    def _():
        stage(n_full - 1, gsub, n_full, gsub, tq)

    @pl.when(jnp.logical_and(some_full, even))
    def _():
        stage(n_full - 1, gsub, n_full, half, 0)

    for parity, n_fin in ((odd, gsub), (even, half)):
        @pl.when(jnp.logical_and(parity, last))
        def _():
            stage(n_full, n_fin, None, 0, None, fin_diag=True)

        @pl.when(jnp.logical_and(parity, inner))
        def _():
            stage(n_full, n_fin, 0, gsub, None, qn_ref, fin_diag=True)

    @pl.when(jnp.logical_and(first, jnp.logical_not(last)))
    def _():
        stage(n_full, half, 0, gsub, tq, qn_ref, fin_diag=True)

    for pr in range(hps // 2):
        ot = jnp.concatenate(
            [acc_ref[hh, :V_HEAD_DIM, :] / acc_ref[hh, V_HEAD_DIM:V_HEAD_DIM + 1, :]
             for hh in (2 * pr, 2 * pr + 1)], axis=0)
        o_ref[0, :, pr * LANES:(pr + 1) * LANES] = ot.T.astype(BF16)


def _attention(qt, k, vt, tq, hps):
    B, H, S, _ = k.shape
    tk = vt.shape[-1]
    resident = pl.Buffered(1)
    nq = S // tq
    return pl.pallas_call(
        functools.partial(_attn_kernel, tq=tq, tk=tk, hps=hps),
        grid=(B, H // hps, nq),
        in_specs=[
            pl.BlockSpec((1, hps, LANES, tq), lambda b, p, i: (b, p, 0, i)),
            pl.BlockSpec((1, hps, LANES, tq), lambda b, p, i: (b, p, 0, jnp.minimum(i + 1, nq - 1))),
            pl.BlockSpec((1, hps, S, LANES), lambda b, p, i: (b, p, 0, 0), pipeline_mode=resident),
            pl.BlockSpec((1, hps, S // tk, LANES, tk), lambda b, p, i: (b, p, 0, 0, 0),
                         pipeline_mode=resident),
        ],
        out_specs=pl.BlockSpec((1, tq, (hps // 2) * LANES), lambda b, p, i: (b, i, p)),
        out_shape=jax.ShapeDtypeStruct((B, S, (H // 2) * LANES), BF16),
        scratch_shapes=[pltpu.VMEM((hps, 1, tq), F32), pltpu.VMEM((hps, 1, tq), F32),
                        pltpu.VMEM((hps, LANES, tq), F32), pltpu.VMEM((2 * tq // tk, hps, tk, tq), F32)],
        compiler_params=_params("arbitrary", "arbitrary", "arbitrary"),
        name="mla_attention",
    )(qt, qt, k, vt)


def _gelu_tanh_of_half(hx):
    c = math.sqrt(2.0 / math.pi)
    inner = hx * ((8.0 * 0.044715 * c) * (hx * hx) + 2.0 * c)
    return hx * (1.0 + jnp.tanh(inner))


def _odd_kernel(x_ref, pos_ref, norm_ref, w_in_ref, conv_w_ref, conv_b_ref, wg_ref, bg_ref,
                lam_ref, w_out_ref, out_ref, ext_ref, hcar_ref):
    i = pl.program_id(1)
    tm = x_ref.shape[1]
    width = lam_ref.shape[1]
    hd = width // LRU_HEADS
    bm = min(ODD_BLOCK_ROWS, tm)
    nblk = tm // bm

    @pl.when(i == 0)
    def _():
        ext_ref[0:CONV_HALO, :] = jnp.zeros((CONV_HALO, width), F32)
        hcar_ref[...] = jnp.zeros(hcar_ref.shape, F32)

    nlam = -lam_ref[...]
    softplus = jnp.maximum(nlam, 0.0) + jnp.log1p(jnp.exp(-jnp.abs(nlam)))
    half_c = (-0.5 * LRU_C) * softplus
    sub = lax.broadcasted_iota(jnp.int32, (8, width), 0)
    keeps = {d: sub >= d for d in (1, 2, 4)}
    hprev = hcar_ref[...]
    def in_proj(blk):
        x = x_ref[0, blk * bm:(blk + 1) * bm, :]
        return x, _dot(_rms(x, norm_ref[...]).astype(BF16), w_in_ref[...])

    ahead = in_proj(0)
    for blk in range(nblk):
        r0 = blk * bm
        x, z = ahead
        if blk + 1 < nblk:
            ahead = in_proj(blk + 1)
        gate = z[:, :width]
        xb = z[:, width:]

        ext_ref[CONV_HALO + r0:CONV_HALO + r0 + bm, :] = xb
        xc = conv_b_ref[...] + conv_w_ref[CONV_WIDTH - 1:CONV_WIDTH, :] * xb
        for k in range(CONV_WIDTH - 1):
            off = CONV_HALO - (CONV_WIDTH - 1) + k + r0
            xc = xc + conv_w_ref[k:k + 1, :] * ext_ref[off:off + bm, :]

        xcb = xc.astype(BF16)
        r_parts, i_parts = [], []
        for h in range(LRU_HEADS):
            g = _dot(xcb[:, h * hd:(h + 1) * hd], wg_ref[h])
            r_parts.append(g[:, :hd])
            i_parts.append(g[:, hd:])
        tr = jnp.tanh(jnp.concatenate(r_parts, axis=-1) + bg_ref[0:1, :])
        ti = jnp.tanh(jnp.concatenate(i_parts, axis=-1) + bg_ref[1:2, :])
        log_a = tr * half_c + half_c
        ig = 0.5 * ti + 0.5
        a = jnp.exp(log_a)
        mult = jnp.sqrt(jnp.maximum(-jnp.tanh(log_a) * (a * a + 1.0), 0.0))
        reset = pos_ref[0, r0:r0 + bm, :] == 0
        a = jnp.where(reset, 0.0, a)
        b = jnp.where(reset, 1.0, mult) * (ig * xc)

        hs = []
        for c in range(bm // 8):
            at = a[c * 8:(c + 1) * 8, :]
            bt = b[c * 8:(c + 1) * 8, :]
            for d in (1, 2, 4):
                a_s = jnp.where(keeps[d], pltpu.roll(at, d, 0), 1.0)
                b_s = jnp.where(keeps[d], pltpu.roll(bt, d, 0), 0.0)
                bt = at * b_s + bt
                at = at * a_s
            hcur = at * hprev + bt
            hs.append(hcur)
            hprev = jnp.broadcast_to(hcur[7:8, :], (8, width))
        y = _gelu_tanh_of_half(gate) * jnp.concatenate(hs, axis=0)
        out_ref[0, r0:r0 + bm, :] = x + _dot(y.astype(BF16), w_out_ref[...])
    ext_ref[0:CONV_HALO, :] = ext_ref[tm:tm + CONV_HALO, :]
    hcar_ref[...] = hprev


def _odd_mixer(x, pos_col, norm, w_in, conv_w, conv_b, wg, bg, lam, w_out, tm):
    B, S, D = x.shape
    width = lam.shape[1]
    const = lambda b, i: (0, 0)
    return pl.pallas_call(
        _odd_kernel,
        grid=(B, S // tm),
        in_specs=[
            pl.BlockSpec((1, tm, D), lambda b, i: (b, i, 0)),
            pl.BlockSpec((1, tm, 1), lambda b, i: (b, i, 0)),
            pl.BlockSpec(norm.shape, const),
            pl.BlockSpec(w_in.shape, const),
            pl.BlockSpec(conv_w.shape, const),
            pl.BlockSpec(conv_b.shape, const),
            pl.BlockSpec(wg.shape, lambda b, i: (0, 0, 0)),
            pl.BlockSpec(bg.shape, const),
            pl.BlockSpec(lam.shape, const),
            pl.BlockSpec(w_out.shape, const),
        ],
        out_specs=pl.BlockSpec((1, tm, D), lambda b, i: (b, i, 0)),
        out_shape=jax.ShapeDtypeStruct((B, S, D), F32),
        scratch_shapes=[pltpu.VMEM((CONV_HALO + tm, width), F32), pltpu.VMEM((8, width), F32)],
        compiler_params=_params("arbitrary", "arbitrary"),
        name="odd_mixer",
    )(x, pos_col, norm, w_in, conv_w, conv_b, wg, bg, lam, w_out)


def _memkv_kernel(mem_ref, norm_ref, w_ref, out_ref):
    out_ref[0] = _dot(_rms(mem_ref[0], norm_ref[...]).astype(BF16), w_ref[...]).astype(BF16)


def _memkv(mem, norm, w_kv, layer):
    B, M, D = mem.shape
    n = w_kv.shape[2]
    return pl.pallas_call(
        _memkv_kernel,
        grid=(B,),
        in_specs=[pl.BlockSpec((1, M, D), lambda b: (b, 0, 0)),
                  pl.BlockSpec(norm.shape, lambda b: (0, 0)),
                  _layer_spec(w_kv, layer)],
        out_specs=pl.BlockSpec((1, M, n), lambda b: (b, 0, 0)),
        out_shape=jax.ShapeDtypeStruct((B, M, n), BF16),
        compiler_params=_params("arbitrary"),
        name="mem_kv",
    )(mem, norm, w_kv)


def _layer_spec(w_all, layer, **kw):
    zeros = (0,) * (w_all.ndim - 1)
    return pl.BlockSpec((None,) + w_all.shape[1:], lambda *_: (layer,) + zeros, **kw)


def _post_kernel(*refs, with_mix, final):
    if with_mix:
        x_ref, yp_ref, ya_ref, wop_ref, woa_ref = refs[:5]
        x = x_ref[...] + _dot(yp_ref[...], wop_ref[...]) + _dot(ya_ref[...], woa_ref[...])
        refs = refs[5:]
    else:
        x = refs[0][...]
        refs = refs[1:]
    xnorm_ref, wq_ref, kv_ref, wo_ref, fnorm_ref, wgu_ref, wd_ref, onorm_ref, out_ref = refs
    d = x.shape[1]
    hd = d // MEM_HEADS
    q = _dot(_rms(x, xnorm_ref[...]).astype(BF16), wq_ref[...]).astype(BF16)
    outs = []
    for h in range(MEM_HEADS):
        kh = kv_ref[0, :, h * hd:(h + 1) * hd]
        vh = kv_ref[0, :, d + h * hd:d + (h + 1) * hd]
        s = _dot_nt(q[:, h * hd:(h + 1) * hd], kh) * (hd ** -0.5)
        p = jnp.exp(s - jnp.max(s, axis=-1, keepdims=True))
        o = _dot(p.astype(BF16), vh) / jnp.sum(p, axis=-1, keepdims=True)
        outs.append(o.astype(BF16))
    x = x + _dot(jnp.concatenate(outs, axis=-1), wo_ref[...])

    ff = wd_ref.shape[0]
    hn = _rms(x, fnorm_ref[...]).astype(BF16)
    g = _dot(hn, wgu_ref[:, :ff])
    u = _dot(hn, wgu_ref[:, ff:])
    act = (g * jax.nn.sigmoid(g) * u).astype(BF16)
    y = x + _dot(act, wd_ref[...])
    out_ref[...] = _rms(y, onorm_ref[...]) if final else y


def _post(x2d, mix, xnorm, wq, kv, wo, fnorm, w_gate_up, w_down, onorm, layer, tm, tiles_per_batch,
          final):
    T, D = x2d.shape
    const = lambda i: (0, 0)
    row = lambda i: (i, 0)
    resident = pl.Buffered(1)
    args = [x2d]
    specs = [pl.BlockSpec((tm, D), row)]
    if mix is not None:
        yp, ya, wop, woa = mix
        args += [yp, ya, wop, woa]
        specs += [pl.BlockSpec((tm, yp.shape[1]), row), pl.BlockSpec((tm, ya.shape[1]), row),
                  pl.BlockSpec(wop.shape, const, pipeline_mode=resident),
                  pl.BlockSpec(woa.shape, const, pipeline_mode=resident)]
    args += [xnorm, wq, kv, wo, fnorm, w_gate_up, w_down, onorm]
    specs += [pl.BlockSpec(xnorm.shape, const),
              _layer_spec(wq, layer, pipeline_mode=resident),
              pl.BlockSpec((1,) + kv.shape[1:], lambda i: (i // tiles_per_batch, 0, 0)),
              _layer_spec(wo, layer, pipeline_mode=resident),
              pl.BlockSpec(fnorm.shape, const),
              _layer_spec(w_gate_up, layer, pipeline_mode=resident),
              _layer_spec(w_down, layer, pipeline_mode=resident),
              pl.BlockSpec(onorm.shape, const)]
    return pl.pallas_call(
        functools.partial(_post_kernel, with_mix=mix is not None, final=final),
        grid=(T // tm,),
        in_specs=specs,
        out_specs=pl.BlockSpec((tm, D), row),
        out_shape=jax.ShapeDtypeStruct((T, D), F32),
        compiler_params=_params("arbitrary"),
        name="xattn_ffn_mix" if mix is not None else "xattn_ffn",
    )(*args)


def _tiles(S):
    tm = min(512, S)
    tq = tm
    tk = min(256, tq)
    tf = min(1024, S)
    return tm, tq, tk, tf


def _pad_cols(w, groups, width):
    k = w.shape[0]
    w = w.reshape(k, groups, -1)
    return jnp.pad(w, ((0, 0), (0, 0), (0, width - w.shape[2]))).reshape(k, groups * width)


def kernel(x, mem, positions, ev_norm, ev_w_in, ev_pool_w, ev_pool_scale, ev_q_norm, ev_w_q_up, ev_kv_norm, ev_w_kv_up, ev_w_out, od_norm, od_w_in, od_conv_w, od_conv_b, od_w_rgate, od_b_rgate, od_w_igate, od_b_igate, od_lambda, od_w_out, xa_norm_x, xa_norm_mem, xa_w_q, xa_w_kv, xa_w_o, ffn_norm, ffn_w_gate_up, ffn_w_down, final_norm):
    B, S, D = x.shape
    depth = xa_w_q.shape[0]
    tm, tq, tk, tf = _tiles(S)
    pos_col = positions.reshape(B, S, 1)
    pool_dim = ev_pool_scale.shape[1]
    q_rank = ev_q_norm.shape[1]
    kv_rank = ev_kv_norm.shape[1]

    pos_row = positions.reshape(B, 1, S)
    inv_freq = ROPE_BASE ** (-jnp.arange(0, QK_ROPE_DIM, 2, dtype=F32) / QK_ROPE_DIM)
    freq_col = inv_freq.reshape(-1, 1)
    ones_col = jnp.zeros((1, LANES), F32).at[0, V_HEAD_DIM].set(1.0)

    row = lambda v: v.reshape(1, -1)
    xa_wq, xa_wkv, xa_wo = (w.astype(BF16) for w in (xa_w_q, xa_w_kv, xa_w_o))
    ffn_wgu, ffn_wd = ffn_w_gate_up.astype(BF16), ffn_w_down.astype(BF16)
    h = x
    for layer in range(depth):
        j = layer // 2
        if layer % 2 == 0:
            lat = pool_dim + q_rank + kv_rank
            w_in = jnp.concatenate(
                [ev_w_in[j][:, :lat], jnp.zeros((D, QK_NOPE_DIM), F32), ev_w_in[j][:, lat:],
                 jnp.zeros((D, LANES - QK_DIM), F32)], axis=1).astype(BF16)
            wq = _pad_cols(ev_w_q_up[j], MLA_HEADS, LANES).astype(BF16)
            wkv = ev_w_kv_up[j].reshape(kv_rank, MLA_HEADS, QK_NOPE_DIM + V_HEAD_DIM)
            wk = _pad_cols(wkv[:, :, :QK_NOPE_DIM].reshape(kv_rank, -1), MLA_HEADS, LANES).astype(BF16)
            wv = _pad_cols(wkv[:, :, QK_NOPE_DIM:].reshape(kv_rank, -1), MLA_HEADS, LANES).astype(BF16)
            y_pool, q, k, v = _even_front(
                h, pos_row, row(ev_norm[j]), w_in, ev_pool_w[j].astype(BF16), row(ev_pool_scale[j]),
                row(ev_q_norm[j]), wq, row(ev_kv_norm[j]), wk, wv, freq_col, ones_col, tf, tk)
            y_att = _attention(q, k, v, tq, ATTN_HEADS_PER_STEP)
            w_out = ev_w_out[j].astype(BF16)
            mix = (y_pool.reshape(B * S, -1), y_att.reshape(B * S, -1),
                   w_out[:pool_dim], w_out[pool_dim:])
            h2d = h.reshape(B * S, D)
        else:
            lw = od_lambda.shape[1]
            wg = (0.5 * jnp.concatenate([od_w_rgate[j], od_w_igate[j]], axis=-1)).astype(BF16)
            bg = 0.5 * jnp.stack([od_b_rgate[j], od_b_igate[j]])
            col_scale = jnp.where(jnp.arange(2 * lw) < lw, 0.5, 1.0).astype(F32)
            h = _odd_mixer(h, pos_col, row(od_norm[j]), (od_w_in[j] * col_scale).astype(BF16), od_conv_w[j],
                           row(od_conv_b[j]), wg, bg, row(od_lambda[j]), od_w_out[j].astype(BF16), tm)
            mix = None
            h2d = h.reshape(B * S, D)
        kv = _memkv(mem, row(xa_norm_mem[layer]), xa_wkv, layer)
        h2d = _post(h2d, mix, row(xa_norm_x[layer]), xa_wq, kv, xa_wo, row(ffn_norm[layer]), ffn_wgu,
                    ffn_wd, row(final_norm), layer, tm, S // tm, final=layer == depth - 1)
        h = h2d.reshape(B, S, D)
    return h
```

```python
import functools
import math

import jax
import jax.numpy as jnp
from jax import lax
from jax.experimental import pallas as pl
from jax.experimental.pallas import tpu as pltpu

F32 = jnp.float32
BF16 = jnp.bfloat16

LANES = 128

POOL_WINDOWS = (2, 4, 8, 16)
POOL_HALO = 16
MLA_HEADS = 8
QK_NOPE_DIM = 64
QK_ROPE_DIM = 32
QK_DIM = QK_NOPE_DIM + QK_ROPE_DIM
V_HEAD_DIM = 64
ATTN_HEADS_PER_STEP = 4
ATTN_STAGES_PER_TRIP = 2
QK_EXP2_SCALE = (QK_DIM ** -0.5) * math.log2(math.e)
ROPE_BASE = 10000.0
LRU_HEADS = 4
CONV_WIDTH = 4
CONV_HALO = 8
ODD_BLOCK_ROWS = 256
LRU_C = 8.0
MEM_HEADS = 4
RMS_EPS = 1e-6
NEG_INF = -1e30

VMEM_LIMIT_BYTES = 56 * 1024 * 1024


def _params(*sem):
    return pltpu.CompilerParams(dimension_semantics=sem, vmem_limit_bytes=VMEM_LIMIT_BYTES)


def _rms(x, g):
    return x * lax.rsqrt(jnp.mean(x * x, axis=-1, keepdims=True) + RMS_EPS) * g


def _dot(a, b):
    return jnp.dot(a, b, preferred_element_type=F32)


def _dot_nt(a, b):
    return lax.dot_general(a, b, (((1,), (1,)), ((), ())), preferred_element_type=F32)


def _even_front_kernel(x_ref, pos_ref, norm_ref, w_in_ref, pool_w_ref, pool_scale_ref,
                       qn_ref, wq_ref, kvn_ref, wk_ref, wv_ref, freq_ref, ones_ref,
                       ypool_ref, q_ref, k_ref, v_ref, ext_ref):
    i = pl.program_id(1)
    tm = x_ref.shape[1]
    pool_dim = ypool_ref.shape[2]
    bm = v_ref.shape[4]
    half = QK_ROPE_DIM // 2
    r1, r2, r3 = QK_NOPE_DIM, QK_NOPE_DIM + half, QK_DIM
    kv0 = pool_dim + qn_ref.shape[1]
    ones_col = ones_ref[...]

    @pl.when(i == 0)
    def _():
        ext_ref[0:POOL_HALO, :] = jnp.zeros((POOL_HALO, pool_dim), F32)

    def in_proj(blk):
        h = _rms(x_ref[0, blk * bm:(blk + 1) * bm, :], norm_ref[...]).astype(BF16)
        return _dot(h, w_in_ref[...])

    ahead = in_proj(0)
    for blk in range(tm // bm):
        r0 = blk * bm
        z = ahead
        if (blk + 1) * bm < tm:
            ahead = in_proj(blk + 1)

        u = z[:, :pool_dim]
        ext_ref[POOL_HALO + r0:POOL_HALO + r0 + bm, :] = u
        t = i * tm + r0 + lax.broadcasted_iota(jnp.int32, (bm, 1), 0)
        parts = []
        for g, w in enumerate(POOL_WINDOWS):
            cols = slice(g * LANES, (g + 1) * LANES)
            acc = ext_ref[r0:r0 + POOL_HALO + bm, cols]
            d = 1
            while d < w:
                acc = acc + pltpu.roll(acc, d, 0)
                d *= 2
            cnt = jnp.minimum(t + 1, w).astype(F32)
            pooled = acc[POOL_HALO:, :] / cnt - u[:, cols]
            parts.append(_dot(pooled.astype(BF16), pool_w_ref[g]))
        y_pool = jnp.concatenate(parts, axis=-1) * pool_scale_ref[...]
        ypool_ref[0, r0:r0 + bm, :] = y_pool.astype(BF16)

        ang = freq_ref[...] * pos_ref[0, :, r0:r0 + bm].astype(F32)
        cos = jnp.cos(ang)
        sin = jnp.sin(ang)

        def rope_t(xt):
            t1, t2 = xt[r1:r2, :], xt[r2:r3, :]
            return jnp.concatenate(
                [xt[:r1, :], t1 * cos - t2 * sin, t2 * cos + t1 * sin, xt[r3:, :]], axis=0)

        qf = _dot(_rms(z[:, pool_dim:kv0], qn_ref[...]).astype(BF16), wq_ref[...])
        for hd in range(MLA_HEADS):
            qt = rope_t(qf[:, hd * LANES:(hd + 1) * LANES].T)
            q_ref[0, hd, :, r0:r0 + bm] = (qt * QK_EXP2_SCALE).astype(BF16)

        kv_lat = _rms(z[:, kv0:kv0 + kvn_ref.shape[1]], kvn_ref[...]).astype(BF16)
        kf = _dot(kv_lat, wk_ref[...])
        vf = _dot(kv_lat, wv_ref[...])
        k_rope = rope_t(z[:, kv0 + kvn_ref.shape[1]:].T).T
        for hd in range(MLA_HEADS):
            k_ref[0, hd, r0:r0 + bm, :] = (kf[:, hd * LANES:(hd + 1) * LANES] + k_rope).astype(BF16)
        for hd in range(MLA_HEADS):
            v_ref[0, hd, blk] = (vf[:, hd * LANES:(hd + 1) * LANES] + ones_col).T.astype(BF16)
    ext_ref[0:POOL_HALO, :] = ext_ref[tm:tm + POOL_HALO, :]


def _even_front(x, pos_row, norm, w_in, pool_w, pool_scale, qn, wq, kvn, wk, wv, freq_col, ones_col,
                tm, tk):
    B, S, D = x.shape
    pool_dim = pool_scale.shape[1]
    const = lambda b, i: (0, 0)
    return pl.pallas_call(
        _even_front_kernel,
        grid=(B, S // tm),
        in_specs=[
            pl.BlockSpec((1, tm, D), lambda b, i: (b, i, 0)),
            pl.BlockSpec((1, 1, tm), lambda b, i: (b, 0, i)),
            pl.BlockSpec(norm.shape, const),
            pl.BlockSpec(w_in.shape, const),
            pl.BlockSpec(pool_w.shape, lambda b, i: (0, 0, 0)),
            pl.BlockSpec(pool_scale.shape, const),
            pl.BlockSpec(qn.shape, const),
            pl.BlockSpec(wq.shape, const),
            pl.BlockSpec(kvn.shape, const),
            pl.BlockSpec(wk.shape, const),
            pl.BlockSpec(wv.shape, const),
            pl.BlockSpec(freq_col.shape, const),
            pl.BlockSpec(ones_col.shape, const),
        ],
        out_specs=[
            pl.BlockSpec((1, tm, pool_dim), lambda b, i: (b, i, 0)),
            pl.BlockSpec((1, MLA_HEADS, LANES, tm), lambda b, i: (b, 0, 0, i)),
            pl.BlockSpec((1, MLA_HEADS, tm, LANES), lambda b, i: (b, 0, i, 0)),
            pl.BlockSpec((1, MLA_HEADS, tm // tk, LANES, tk), lambda b, i: (b, 0, i, 0, 0)),
        ],
        out_shape=[
            jax.ShapeDtypeStruct((B, S, pool_dim), BF16),
            jax.ShapeDtypeStruct((B, MLA_HEADS, LANES, S), BF16),
            jax.ShapeDtypeStruct((B, MLA_HEADS, S, LANES), BF16),
            jax.ShapeDtypeStruct((B, MLA_HEADS, S // tk, LANES, tk), BF16),
        ],
        scratch_shapes=[pltpu.VMEM((POOL_HALO + tm, pool_dim), F32)],
        compiler_params=_params("arbitrary", "arbitrary"),
        name="even_front",
    )(x, pos_row, norm, w_in, pool_w, pool_scale, qn, wq, kvn, wk, wv, freq_col, ones_col)


def _attn_kernel(qt_ref, qn_ref, k_ref, vt_ref, o_ref, m_ref, mt_ref, acc_ref, s_ref, *, tq, tk, hps):
    qi = pl.program_id(2)
    gsub = s_ref.shape[0]
    assert gsub * tk == 2 * tq
    m_ref[...] = jnp.full(m_ref.shape, NEG_INF, F32)
    acc_ref[...] = jnp.zeros(acc_ref.shape, F32)

    hq = tq // 2

    def stage(g_fin, n_fin, g_new, n_new, mask_off, q_ref=qt_ref, fin_diag=False):
        for hh in range(hps):
            if n_fin:
                m_old = m_ref[hh]
                m_new = jnp.maximum(m_old, mt_ref[hh])
                alpha = jnp.exp2(m_old - m_new)
            pv = None
            mt = None
            for u in range(max(n_fin, n_new)):
                if u < n_fin:
                    vt = vt_ref[0, hh, g_fin * gsub + u]
                    if fin_diag and u == n_fin - 1:
                        p = jnp.exp2(s_ref[u, hh, :, hq:] - m_new[:, hq:]).astype(BF16)
                        d = jnp.concatenate([jnp.zeros((LANES, hq), F32), _dot(vt, p)], axis=1)
                    else:
                        d = _dot(vt, jnp.exp2(s_ref[u, hh] - m_new).astype(BF16))
                    pv = d if pv is None else pv + d
                if u < n_new:
                    r0 = pl.multiple_of((g_new * gsub + u) * tk, tk)
                    kt = k_ref[0, hh, pl.ds(r0, tk), :]
                    if mask_off is not None and u == n_new - 1:
                        assert u * tk - mask_off == hq
                        s = _dot(kt, q_ref[0, hh, :, hq:])
                        kk = lax.broadcasted_iota(jnp.int32, (tk, hq), 0)
                        qq = lax.broadcasted_iota(jnp.int32, (tk, hq), 1)
                        s = jnp.where(kk <= qq, s, NEG_INF)
                        s_ref[u, hh, :, hq:] = s
                        cm = jnp.concatenate([jnp.full((1, hq), NEG_INF, F32),
                                              jnp.max(s, axis=0, keepdims=True)], axis=1)
                    else:
                        s = _dot(kt, q_ref[0, hh])
                        if mask_off is not None:
                            kk = lax.broadcasted_iota(jnp.int32, (tk, tq), 0) + (u * tk - mask_off)
                            qq = lax.broadcasted_iota(jnp.int32, (tk, tq), 1)
                            s = jnp.where(kk <= qq, s, NEG_INF)
                        s_ref[u, hh] = s
                        cm = jnp.max(s, axis=0, keepdims=True)
                    mt = cm if mt is None else jnp.maximum(mt, cm)
            if n_fin:
                acc_ref[hh] = alpha * acc_ref[hh] + pv
                m_ref[hh] = m_new
            if n_new:
                mt_ref[hh] = mt

    n_full = qi // 2
    odd = qi % 2 == 1
    even = jnp.logical_not(odd)
    some_full = n_full > 0
    first = qi == 0
    last = qi == pl.num_programs(2) - 1
    inner = jnp.logical_not(jnp.logical_or(first, last))
    half = gsub // 2

    @pl.when(first)
    def _():
        stage(None, 0, 0, half, 0)

    def body(t, carry):
        for k in range(ATTN_STAGES_PER_TRIP):
            g = ATTN_STAGES_PER_TRIP * t + k
            stage(g, gsub, g + 1, gsub, None)
        return carry

    def rest(g, carry):
        stage(g, gsub, g + 1, gsub, None)
        return carry

    steady = jnp.maximum(n_full - 1, 0)
    trips = steady // ATTN_STAGES_PER_TRIP
    lax.fori_loop(0, trips, body, 0)
    lax.fori_loop(ATTN_STAGES_PER_TRIP * trips, steady, rest, 0)

    @pl.when(jnp.logical_and(some_full, odd))
    def _():
        stage(n_full - 1, gsub, n_full, gsub, tq)

    @pl.when(jnp.logical_and(some_full, even))
    def _():
        stage(n_full - 1, gsub, n_full, half, 0)

    for parity, n_fin in ((odd, gsub), (even, half)):
        @pl.when(jnp.logical_and(parity, last))
        def _():
            stage(n_full, n_fin, None, 0, None, fin_diag=True)

        @pl.when(jnp.logical_and(parity, inner))
        def _():
            stage(n_full, n_fin, 0, gsub, None, qn_ref, fin_diag=True)

    @pl.when(jnp.logical_and(first, jnp.logical_not(last)))
    def _():
        stage(n_full, half, 0, gsub, tq, qn_ref, fin_diag=True)

    for pr in range(hps // 2):
        ot = jnp.concatenate(
            [acc_ref[hh, :V_HEAD_DIM, :] / acc_ref[hh, V_HEAD_DIM:V_HEAD_DIM + 1, :]
             for hh in (2 * pr, 2 * pr + 1)], axis=0)
        o_ref[0, :, pr * LANES:(pr + 1) * LANES] = ot.T.astype(BF16)


def _attention(qt, k, vt, tq, hps):
    B, H, S, _ = k.shape
    tk = vt.shape[-1]
    resident = pl.Buffered(1)
    nq = S // tq
    return pl.pallas_call(
        functools.partial(_attn_kernel, tq=tq, tk=tk, hps=hps),
        grid=(B, H // hps, nq),
        in_specs=[
            pl.BlockSpec((1, hps, LANES, tq), lambda b, p, i: (b, p, 0, i)),
            pl.BlockSpec((1, hps, LANES, tq), lambda b, p, i: (b, p, 0, jnp.minimum(i + 1, nq - 1))),
            pl.BlockSpec((1, hps, S, LANES), lambda b, p, i: (b, p, 0, 0), pipeline_mode=resident),
            pl.BlockSpec((1, hps, S // tk, LANES, tk), lambda b, p, i: (b, p, 0, 0, 0),
                         pipeline_mode=resident),
        ],
        out_specs=pl.BlockSpec((1, tq, (hps // 2) * LANES), lambda b, p, i: (b, i, p)),
        out_shape=jax.ShapeDtypeStruct((B, S, (H // 2) * LANES), BF16),
        scratch_shapes=[pltpu.VMEM((hps, 1, tq), F32), pltpu.VMEM((hps, 1, tq), F32),
                        pltpu.VMEM((hps, LANES, tq), F32), pltpu.VMEM((2 * tq // tk, hps, tk, tq), F32)],
        compiler_params=_params("arbitrary", "arbitrary", "arbitrary"),
        name="mla_attention",
    )(qt, qt, k, vt)


def _gelu_tanh_of_half(hx):
    c = math.sqrt(2.0 / math.pi)
    inner = hx * ((8.0 * 0.044715 * c) * (hx * hx) + 2.0 * c)
    return hx * (1.0 + jnp.tanh(inner))


def _odd_kernel(x_ref, pos_ref, norm_ref, w_in_ref, conv_w_ref, conv_b_ref, wg_ref, bg_ref,
                lam_ref, w_out_ref, out_ref, ext_ref, hcar_ref):
    i = pl.program_id(1)
    tm = x_ref.shape[1]
    width = lam_ref.shape[1]
    hd = width // LRU_HEADS
    bm = min(ODD_BLOCK_ROWS, tm)
    nblk = tm // bm

    @pl.when(i == 0)
    def _():
        ext_ref[0:CONV_HALO, :] = jnp.zeros((CONV_HALO, width), F32)
        hcar_ref[...] = jnp.zeros(hcar_ref.shape, F32)

    nlam = -lam_ref[...]
    softplus = jnp.maximum(nlam, 0.0) + jnp.log1p(jnp.exp(-jnp.abs(nlam)))
    half_c = (-0.5 * LRU_C) * softplus
    sub = lax.broadcasted_iota(jnp.int32, (8, width), 0)
    keeps = {d: sub >= d for d in (1, 2, 4)}
    hprev = hcar_ref[...]
    def in_proj(blk):
        x = x_ref[0, blk * bm:(blk + 1) * bm, :]
        return x, _dot(_rms(x, norm_ref[...]).astype(BF16), w_in_ref[...])

    ahead = in_proj(0)
    for blk in range(nblk):
        r0 = blk * bm
        x, z = ahead
        if blk + 1 < nblk:
            ahead = in_proj(blk + 1)
        gate = z[:, :width]
        xb = z[:, width:]

        ext_ref[CONV_HALO + r0:CONV_HALO + r0 + bm, :] = xb
        xc = conv_b_ref[...] + conv_w_ref[CONV_WIDTH - 1:CONV_WIDTH, :] * xb
        for k in range(CONV_WIDTH - 1):
            off = CONV_HALO - (CONV_WIDTH - 1) + k + r0
            xc = xc + conv_w_ref[k:k + 1, :] * ext_ref[off:off + bm, :]

        xcb = xc.astype(BF16)
        r_parts, i_parts = [], []
        for h in range(LRU_HEADS):
            g = _dot(xcb[:, h * hd:(h + 1) * hd], wg_ref[h])
            r_parts.append(g[:, :hd])
            i_parts.append(g[:, hd:])
        tr = jnp.tanh(jnp.concatenate(r_parts, axis=-1) + bg_ref[0:1, :])
        ti = jnp.tanh(jnp.concatenate(i_parts, axis=-1) + bg_ref[1:2, :])
        log_a = tr * half_c + half_c
        ig = 0.5 * ti + 0.5
        a = jnp.exp(log_a)
        mult = jnp.sqrt(jnp.maximum(-jnp.tanh(log_a) * (a * a + 1.0), 0.0))
        reset = pos_ref[0, r0:r0 + bm, :] == 0
        a = jnp.where(reset, 0.0, a)
        b = jnp.where(reset, 1.0, mult) * (ig * xc)

        hs = []
        for c in range(bm // 8):
            at = a[c * 8:(c + 1) * 8, :]
            bt = b[c * 8:(c + 1) * 8, :]
            for d in (1, 2, 4):
                a_s = jnp.where(keeps[d], pltpu.roll(at, d, 0), 1.0)
                b_s = jnp.where(keeps[d], pltpu.roll(bt, d, 0), 0.0)
                bt = at * b_s + bt
                at = at * a_s
            hcur = at * hprev + bt
            hs.append(hcur)
            hprev = jnp.broadcast_to(hcur[7:8, :], (8, width))
        y = _gelu_tanh_of_half(gate) * jnp.concatenate(hs, axis=0)
        out_ref[0, r0:r0 + bm, :] = x + _dot(y.astype(BF16), w_out_ref[...])
    ext_ref[0:CONV_HALO, :] = ext_ref[tm:tm + CONV_HALO, :]
    hcar_ref[...] = hprev


def _odd_mixer(x, pos_col, norm, w_in, conv_w, conv_b, wg, bg, lam, w_out, tm):
    B, S, D = x.shape
    width = lam.shape[1]
    const = lambda b, i: (0, 0)
    return pl.pallas_call(
        _odd_kernel,
        grid=(B, S // tm),
        in_specs=[
            pl.BlockSpec((1, tm, D), lambda b, i: (b, i, 0)),
            pl.BlockSpec((1, tm, 1), lambda b, i: (b, i, 0)),
            pl.BlockSpec(norm.shape, const),
            pl.BlockSpec(w_in.shape, const),
            pl.BlockSpec(conv_w.shape, const),
            pl.BlockSpec(conv_b.shape, const),
            pl.BlockSpec(wg.shape, lambda b, i: (0, 0, 0)),
            pl.BlockSpec(bg.shape, const),
            pl.BlockSpec(lam.shape, const),
            pl.BlockSpec(w_out.shape, const),
        ],
        out_specs=pl.BlockSpec((1, tm, D), lambda b, i: (b, i, 0)),
        out_shape=jax.ShapeDtypeStruct((B, S, D), F32),
        scratch_shapes=[pltpu.VMEM((CONV_HALO + tm, width), F32), pltpu.VMEM((8, width), F32)],
        compiler_params=_params("arbitrary", "arbitrary"),
        name="odd_mixer",
    )(x, pos_col, norm, w_in, conv_w, conv_b, wg, bg, lam, w_out)


def _layer_spec(w_all, layer, **kw):
    zeros = (0,) * (w_all.ndim - 1)
    return pl.BlockSpec((None,) + w_all.shape[1:], lambda *_: (layer,) + zeros, **kw)


def _post_kernel(*refs, with_mix, final, tiles_per_batch):
    if with_mix:
        x_ref, yp_ref, ya_ref, wop_ref, woa_ref = refs[:5]
        x = x_ref[...] + _dot(yp_ref[...], wop_ref[...]) + _dot(ya_ref[...], woa_ref[...])
        refs = refs[5:]
    else:
        x = refs[0][...]
        refs = refs[1:]
    (xnorm_ref, wq_ref, mem_ref, mnorm_ref, wkv_ref, wo_ref, fnorm_ref, wgu_ref, wd_ref, onorm_ref,
     out_ref, kv_ref) = refs
    d = x.shape[1]
    hd = d // MEM_HEADS

    @pl.when(pl.program_id(0) % tiles_per_batch == 0)
    def _():
        m = _rms(mem_ref[0], mnorm_ref[...]).astype(BF16)
        kv_ref[...] = _dot(m, wkv_ref[...]).astype(BF16)

    q = _dot(_rms(x, xnorm_ref[...]).astype(BF16), wq_ref[...]).astype(BF16)
    outs = []
    for h in range(MEM_HEADS):
        kh = kv_ref[:, h * hd:(h + 1) * hd]
        vh = kv_ref[:, d + h * hd:d + (h + 1) * hd]
        s = _dot_nt(q[:, h * hd:(h + 1) * hd], kh) * (hd ** -0.5)
        p = jnp.exp(s - jnp.max(s, axis=-1, keepdims=True))
        o = _dot(p.astype(BF16), vh) / jnp.sum(p, axis=-1, keepdims=True)
        outs.append(o.astype(BF16))
    x = x + _dot(jnp.concatenate(outs, axis=-1), wo_ref[...])

    ff = wd_ref.shape[0]
    hn = _rms(x, fnorm_ref[...]).astype(BF16)
    g = _dot(hn, wgu_ref[:, :ff])
    u = _dot(hn, wgu_ref[:, ff:])
    act = (g * jax.nn.sigmoid(g) * u).astype(BF16)
    y = x + _dot(act, wd_ref[...])
    out_ref[...] = _rms(y, onorm_ref[...]) if final else y


def _post(x2d, mix, xnorm, wq, mem, mnorm, wkv, wo, fnorm, w_gate_up, w_down, onorm, layer, tm,
          tiles_per_batch, final):
    T, D = x2d.shape
    const = lambda i: (0, 0)
    row = lambda i: (i, 0)
    resident = pl.Buffered(1)
    args = [x2d]
    specs = [pl.BlockSpec((tm, D), row)]
    if mix is not None:
        yp, ya, wop, woa = mix
        args += [yp, ya, wop, woa]
        specs += [pl.BlockSpec((tm, yp.shape[1]), row), pl.BlockSpec((tm, ya.shape[1]), row),
                  pl.BlockSpec(wop.shape, const, pipeline_mode=resident),
                  pl.BlockSpec(woa.shape, const, pipeline_mode=resident)]
    args += [xnorm, wq, mem, mnorm, wkv, wo, fnorm, w_gate_up, w_down, onorm]
    specs += [pl.BlockSpec(xnorm.shape, const),
              _layer_spec(wq, layer, pipeline_mode=resident),
              pl.BlockSpec((1,) + mem.shape[1:], lambda i: (i // tiles_per_batch, 0, 0)),
              pl.BlockSpec(mnorm.shape, const),
              _layer_spec(wkv, layer, pipeline_mode=resident),
              _layer_spec(wo, layer, pipeline_mode=resident),
              pl.BlockSpec(fnorm.shape, const),
              _layer_spec(w_gate_up, layer, pipeline_mode=resident),
              _layer_spec(w_down, layer, pipeline_mode=resident),
              pl.BlockSpec(onorm.shape, const)]
    return pl.pallas_call(
        functools.partial(_post_kernel, with_mix=mix is not None, final=final,
                          tiles_per_batch=tiles_per_batch),
        grid=(T // tm,),
        in_specs=specs,
        out_specs=pl.BlockSpec((tm, D), row),
        out_shape=jax.ShapeDtypeStruct((T, D), F32),
        scratch_shapes=[pltpu.VMEM((mem.shape[1], wkv.shape[2]), BF16)],
        compiler_params=_params("arbitrary"),
        name="xattn_ffn_mix" if mix is not None else "xattn_ffn",
    )(*args)


def _tiles(S):
    tm = min(512, S)
    tq = tm
    tk = min(256, tq)
    tf = min(1024, S)
    return tm, tq, tk, tf


def _pad_cols(w, groups, width):
    k = w.shape[0]
    w = w.reshape(k, groups, -1)
    return jnp.pad(w, ((0, 0), (0, 0), (0, width - w.shape[2]))).reshape(k, groups * width)


def kernel(x, mem, positions, ev_norm, ev_w_in, ev_pool_w, ev_pool_scale, ev_q_norm, ev_w_q_up, ev_kv_norm, ev_w_kv_up, ev_w_out, od_norm, od_w_in, od_conv_w, od_conv_b, od_w_rgate, od_b_rgate, od_w_igate, od_b_igate, od_lambda, od_w_out, xa_norm_x, xa_norm_mem, xa_w_q, xa_w_kv, xa_w_o, ffn_norm, ffn_w_gate_up, ffn_w_down, final_norm):
    B, S, D = x.shape
    depth = xa_w_q.shape[0]
    tm, tq, tk, tf = _tiles(S)
    pos_col = positions.reshape(B, S, 1)
    pool_dim = ev_pool_scale.shape[1]
    q_rank = ev_q_norm.shape[1]
    kv_rank = ev_kv_norm.shape[1]

    pos_row = positions.reshape(B, 1, S)
    inv_freq = ROPE_BASE ** (-jnp.arange(0, QK_ROPE_DIM, 2, dtype=F32) / QK_ROPE_DIM)
    freq_col = inv_freq.reshape(-1, 1)
    ones_col = jnp.zeros((1, LANES), F32).at[0, V_HEAD_DIM].set(1.0)

    row = lambda v: v.reshape(1, -1)
    xa_wq, xa_wkv, xa_wo = (w.astype(BF16) for w in (xa_w_q, xa_w_kv, xa_w_o))
    ffn_wgu, ffn_wd = ffn_w_gate_up.astype(BF16), ffn_w_down.astype(BF16)
    h = x
    for layer in range(depth):
        j = layer // 2
        if layer % 2 == 0:
            lat = pool_dim + q_rank + kv_rank
            w_in = jnp.concatenate(
                [ev_w_in[j][:, :lat], jnp.zeros((D, QK_NOPE_DIM), F32), ev_w_in[j][:, lat:],
                 jnp.zeros((D, LANES - QK_DIM), F32)], axis=1).astype(BF16)
            wq = _pad_cols(ev_w_q_up[j], MLA_HEADS, LANES).astype(BF16)
            wkv = ev_w_kv_up[j].reshape(kv_rank, MLA_HEADS, QK_NOPE_DIM + V_HEAD_DIM)
            wk = _pad_cols(wkv[:, :, :QK_NOPE_DIM].reshape(kv_rank, -1), MLA_HEADS, LANES).astype(BF16)
            wv = _pad_cols(wkv[:, :, QK_NOPE_DIM:].reshape(kv_rank, -1), MLA_HEADS, LANES).astype(BF16)
            y_pool, q, k, v = _even_front(
                h, pos_row, row(ev_norm[j]), w_in, ev_pool_w[j].astype(BF16), row(ev_pool_scale[j]),
                row(ev_q_norm[j]), wq, row(ev_kv_norm[j]), wk, wv, freq_col, ones_col, tf, tk)
            y_att = _attention(q, k, v, tq, ATTN_HEADS_PER_STEP)
            w_out = ev_w_out[j].astype(BF16)
            mix = (y_pool.reshape(B * S, -1), y_att.reshape(B * S, -1),
                   w_out[:pool_dim], w_out[pool_dim:])
            h2d = h.reshape(B * S, D)
        else:
            lw = od_lambda.shape[1]
            wg = (0.5 * jnp.concatenate([od_w_rgate[j], od_w_igate[j]], axis=-1)).astype(BF16)
            bg = 0.5 * jnp.stack([od_b_rgate[j], od_b_igate[j]])
            col_scale = jnp.where(jnp.arange(2 * lw) < lw, 0.5, 1.0).astype(F32)
            h = _odd_mixer(h, pos_col, row(od_norm[j]), (od_w_in[j] * col_scale).astype(BF16), od_conv_w[j],
                           row(od_conv_b[j]), wg, bg, row(od_lambda[j]), od_w_out[j].astype(BF16), tm)
            mix = None
            h2d = h.reshape(B * S, D)
        h2d = _post(h2d, mix, row(xa_norm_x[layer]), xa_wq, mem, row(xa_norm_mem[layer]), xa_wkv, xa_wo,
                    row(ffn_norm[layer]), ffn_wgu, ffn_wd, row(final_norm), layer, tm, S // tm,
                    final=layer == depth - 1)
        h = h2d.reshape(B, S, D)
    return h
```
